```python
import math
import jax, jax.numpy as jnp
from jax import lax
import numpy as np

D_MODEL = 4096
BATCH = 4
SEQ = 2048
DEPTH = 2
DEC_BATCH = 128
DEC_SEQ = 4
PAST_LEN = 16384
PAGE_SIZE = 128

M_HEADS = 8
M_DQK = D_MODEL // 16
M_DV = D_MODEL // 8
M_QK_WIDTH = M_HEADS * M_DQK
M_WIDTH = M_HEADS * M_DV
M_CHUNK = 128
POOL_WINDOWS = (2, 4, 8, 16)
POOL_GROUPS = 4
POOL_WIDTH = D_MODEL // 2
POOL_GW = POOL_WIDTH // POOL_GROUPS
POOL_BUF = max(POOL_WINDOWS) - 1
CONV_WIDTH = D_MODEL // 2
CONV_K = 3
N_BRANCH = 3
D_FF = -(-8 * D_MODEL // (3 * 256)) * 256

OFF_Q = 0
OFF_K = OFF_Q + M_QK_WIDTH
OFF_V = OFF_K + M_QK_WIDTH
OFF_O = OFF_V + M_WIDTH
OFF_I = OFF_O + M_WIDTH
OFF_F = OFF_I + M_HEADS
OFF_P = OFF_F + M_HEADS
OFF_CB = OFF_P + POOL_WIDTH
OFF_CC = OFF_CB + CONV_WIDTH
OFF_CX = OFF_CC + CONV_WIDTH
OFF_G = OFF_CX + CONV_WIDTH
D_IN = OFF_G + N_BRANCH * D_MODEL

ALPHA = (2 * DEPTH) ** 0.25
BETA = (8 * DEPTH) ** -0.25
LN_EPS = 1e-5

kernel_name = 'hybrid_mlstm_pool_shortconv_decoder_step'


def _layernorm(x, g, b):
    xf = x.astype(jnp.float32)
    mu = xf.mean(-1, keepdims=True)
    var = jnp.square(xf - mu).mean(-1, keepdims=True)
    return ((xf - mu) * lax.rsqrt(var + LN_EPS) * g + b).astype(x.dtype)


def _mlstm_chunk(carry, inp):
    C, n, m = carry
    q, k, v, logi, logf = inp
    L = q.shape[2]
    b = jnp.cumsum(logf, axis=-1)
    causal = jnp.tril(jnp.ones((L, L), dtype=bool))
    dmat = b[..., :, None] - b[..., None, :] + logi[..., None, :]
    dmat = jnp.where(causal, dmat, -jnp.inf)
    inter = b + m[..., None]
    m_t = jnp.maximum(inter, dmat.max(-1))
    w_inter = jnp.exp(inter - m_t)
    s = jnp.einsum('bhtd,bhsd->bhts', q, k) * jnp.exp(dmat - m_t[..., None])
    num = (w_inter[..., None] * jnp.einsum('bhvd,bhtd->bhtv', C, q)
           + jnp.einsum('bhts,bhsv->bhtv', s, v))
    den = w_inter * jnp.einsum('bhd,bhtd->bht', n, q) + s.sum(-1)
    h = num / jnp.maximum(jnp.abs(den), jnp.exp(-m_t))[..., None]
    m_new = m_t[..., -1]
    w_c = jnp.exp(inter[..., -1] - m_new)
    w_s = jnp.exp(dmat[..., -1, :] - m_new[..., None])
    C_new = w_c[..., None, None] * C + jnp.einsum('bhsv,bhsd->bhvd', w_s[..., None] * v, k)
    n_new = w_c[..., None] * n + jnp.einsum('bhs,bhsd->bhd', w_s, k)
    return (C_new, n_new, m_new), h


def _mlstm(q, k, v, logi, logf, C0, n0, m0):
    Bsz, L = q.shape[0], q.shape[1]
    chunk = M_CHUNK if L % M_CHUNK == 0 else L
    nc = L // chunk

    def to_chunks(a):
        a = a.astype(jnp.float32).reshape((Bsz, nc, chunk) + a.shape[2:])
        return jnp.swapaxes(jnp.moveaxis(a, 1, 0), 2, 3)

    xs = (to_chunks(q), to_chunks(k), to_chunks(v), to_chunks(logi), to_chunks(logf))
    carry0 = (C0.astype(jnp.float32), n0.astype(jnp.float32), m0.astype(jnp.float32))
    (C, n, m), h = lax.scan(_mlstm_chunk, carry0, xs)
    h = jnp.swapaxes(jnp.moveaxis(h, 0, 1), 2, 3).reshape(Bsz, L, M_HEADS, M_DV)
    return h, C, n, m


def _pool(u, buf, start_pos, pool_maps, pool_scale):
    Bsz, L, _ = u.shape
    full = jnp.concatenate([buf.astype(u.dtype), u], axis=1)
    cs = jnp.cumsum(full.astype(jnp.float32), axis=1)
    cs = jnp.concatenate([jnp.zeros_like(cs[:, :1]), cs], axis=1)
    pos = start_pos + jnp.arange(L)
    means = []
    for g, w in enumerate(POOL_WINDOWS):
        sl = slice(g * POOL_GW, (g + 1) * POOL_GW)
        hi = cs[:, POOL_BUF + 1:POOL_BUF + 1 + L, sl]
        lo = cs[:, POOL_BUF + 1 - w:POOL_BUF + 1 - w + L, sl]
        cnt = jnp.minimum(pos + 1, w).astype(jnp.float32)
        means.append((hi - lo) / cnt[None, :, None])
    zf = jnp.concatenate(means, axis=-1) - u.astype(jnp.float32)
    zf = zf.reshape(Bsz, L, POOL_GROUPS, POOL_GW)
    zf = jnp.einsum('blgc,gcd->blgd', zf, pool_maps).reshape(Bsz, L, POOL_WIDTH) * pool_scale
    return zf.astype(u.dtype), full[:, -POOL_BUF:]


def _short_conv(cb, cc, cx, buf, conv_w):
    L = cx.shape[1]
    full = jnp.concatenate([buf.astype(cx.dtype), cc * cx], axis=1)
    y = full[:, 0:L] * conv_w[0]
    for j in range(1, CONV_K):
        y = y + full[:, j:j + L] * conv_w[j]
    return cb * y, full[:, -(CONV_K - 1):]


def _layer(x, C0, n0, m0, pbuf, cbuf, start_pos, w_in, b_in, mh_g, pool_maps, pool_scale,
           conv_w, w_bm, w_bp, w_bs, w_o, ln1_g, ln1_b, w_gate, w_up, w_down, ln2_g, ln2_b):
    Bsz, L, _ = x.shape
    z = jnp.einsum('bld,de->ble', x, w_in) + b_in
    q = z[..., OFF_Q:OFF_K].reshape(Bsz, L, M_HEADS, M_DQK) * (M_DQK ** -0.5)
    k = z[..., OFF_K:OFF_V].reshape(Bsz, L, M_HEADS, M_DQK)
    v = z[..., OFF_V:OFF_O].reshape(Bsz, L, M_HEADS, M_DV)
    o_gate = jax.nn.sigmoid(z[..., OFF_O:OFF_I].astype(jnp.float32))
    logi = z[..., OFF_I:OFF_F].astype(jnp.float32)
    logf = jax.nn.log_sigmoid(z[..., OFF_F:OFF_P].astype(jnp.float32))
    h, C, n, m = _mlstm(q, k, v, logi, logf, C0, n0, m0)
    mu = h.mean(-1, keepdims=True)
    var = jnp.square(h - mu).mean(-1, keepdims=True)
    h = ((h - mu) * lax.rsqrt(var + LN_EPS)).reshape(Bsz, L, M_WIDTH) * mh_g
    y_m = jnp.einsum('ble,ed->bld', (o_gate * h).astype(x.dtype), w_bm)
    zp, pbuf_new = _pool(z[..., OFF_P:OFF_CB], pbuf, start_pos, pool_maps, pool_scale)
    y_p = jnp.einsum('ble,ed->bld', zp, w_bp)
    zs, cbuf_new = _short_conv(z[..., OFF_CB:OFF_CC], z[..., OFF_CC:OFF_CX],
                               z[..., OFF_CX:OFF_G], cbuf, conv_w)
    y_s = jnp.einsum('ble,ed->bld', zs, w_bs)
    gates = jax.nn.sigmoid(z[..., OFF_G:]).reshape(Bsz, L, N_BRANCH, D_MODEL)
    mixed = gates[:, :, 0] * y_m + gates[:, :, 1] * y_p + gates[:, :, 2] * y_s
    x = _layernorm(ALPHA * x + jnp.einsum('bld,de->ble', mixed, w_o), ln1_g, ln1_b)
    ff = jax.nn.silu(jnp.einsum('bld,df->blf', x, w_gate)) * jnp.einsum('bld,df->blf', x, w_up)
    x = _layernorm(ALPHA * x + jnp.einsum('blf,fd->bld', ff, w_down), ln2_g, ln2_b)
    return x, C, n, m, pbuf_new, cbuf_new


def setup_inputs(seed: int = 0) -> dict:
    key = jax.random.key(seed)
    ks = jax.random.split(key, 24)
    f32 = jnp.float32

    def nrm(k, shape, scale):
        return scale * jax.random.normal(k, shape, f32)

    b_in = nrm(ks[8], (DEPTH, D_IN), 0.02)
    b_in = b_in.at[:, OFF_F:OFF_P].add(jnp.linspace(3.0, 6.0, M_HEADS, dtype=f32))
    return {
        'x_prompt': nrm(ks[0], (BATCH, SEQ, D_MODEL), 1.0),
        'x_sample': nrm(ks[1], (DEC_BATCH, DEC_SEQ, D_MODEL), 1.0),
        'state_C': nrm(ks[2], (DEPTH, DEC_BATCH, M_HEADS, M_DV, M_DQK), 0.1),
        'state_n': nrm(ks[3], (DEPTH, DEC_BATCH, M_HEADS, M_DQK), 0.1),
        'state_m': nrm(ks[4], (DEPTH, DEC_BATCH, M_HEADS), 1.0),
        'state_pool': nrm(ks[5], (DEPTH, DEC_BATCH, POOL_BUF, POOL_WIDTH), 1.0),
        'state_conv': nrm(ks[6], (DEPTH, DEC_BATCH, CONV_K - 1, CONV_WIDTH), 1.0),
        'w_in': nrm(ks[7], (DEPTH, D_MODEL, D_IN), D_MODEL ** -0.5),
        'b_in': b_in,
        'mh_g': 1.0 + nrm(ks[9], (DEPTH, M_WIDTH), 0.02),
        'pool_maps': nrm(ks[10], (DEPTH, POOL_GROUPS, POOL_GW, POOL_GW), POOL_GW ** -0.5),
        'pool_scale': 1.0 + nrm(ks[11], (DEPTH, POOL_WIDTH), 0.02),
        'conv_w': nrm(ks[12], (DEPTH, CONV_K, CONV_WIDTH), CONV_K ** -0.5),
        'w_bm': nrm(ks[13], (DEPTH, M_WIDTH, D_MODEL), M_WIDTH ** -0.5),
        'w_bp': nrm(ks[14], (DEPTH, POOL_WIDTH, D_MODEL), POOL_WIDTH ** -0.5),
        'w_bs': nrm(ks[15], (DEPTH, CONV_WIDTH, D_MODEL), CONV_WIDTH ** -0.5),
        'w_o': nrm(ks[16], (DEPTH, D_MODEL, D_MODEL), BETA * D_MODEL ** -0.5),
        'ln1_g': 1.0 + nrm(ks[17], (DEPTH, D_MODEL), 0.02),
        'ln1_b': nrm(ks[18], (DEPTH, D_MODEL), 0.02),
        'w_gate': nrm(ks[19], (DEPTH, D_MODEL, D_FF), D_MODEL ** -0.5),
        'w_up': nrm(ks[20], (DEPTH, D_MODEL, D_FF), D_MODEL ** -0.5),
        'w_down': nrm(ks[21], (DEPTH, D_FF, D_MODEL), BETA * D_FF ** -0.5),
        'ln2_g': 1.0 + nrm(ks[22], (DEPTH, D_MODEL), 0.02),
        'ln2_b': nrm(ks[23], (DEPTH, D_MODEL), 0.02),
    }


def reference(x_prompt, x_sample, state_C, state_n, state_m, state_pool, state_conv,
              w_in, b_in, mh_g, pool_maps, pool_scale, conv_w, w_bm, w_bp, w_bs, w_o,
              ln1_g, ln1_b, w_gate, w_up, w_down, ln2_g, ln2_b):
    bp = x_prompt.shape[0]
    f32 = jnp.float32
    yp, ys = x_prompt, x_sample
    pC, pn, pm, pp, pc = [], [], [], [], []
    sC, sn, sm, sp, sc = [], [], [], [], []
    for l in range(DEPTH):
        wl = (w_in[l], b_in[l], mh_g[l], pool_maps[l], pool_scale[l], conv_w[l], w_bm[l],
              w_bp[l], w_bs[l], w_o[l], ln1_g[l], ln1_b[l], w_gate[l], w_up[l], w_down[l],
              ln2_g[l], ln2_b[l])
        yp, C, n, m, pb, cb = _layer(
            yp, jnp.zeros((bp, M_HEADS, M_DV, M_DQK), f32), jnp.zeros((bp, M_HEADS, M_DQK), f32),
            jnp.zeros((bp, M_HEADS), f32), jnp.zeros((bp, POOL_BUF, POOL_WIDTH), yp.dtype),
            jnp.zeros((bp, CONV_K - 1, CONV_WIDTH), yp.dtype), 0, *wl)
        pC.append(C); pn.append(n); pm.append(m); pp.append(pb); pc.append(cb)
        ys, C, n, m, pb, cb = _layer(
            ys, state_C[l], state_n[l], state_m[l], state_pool[l], state_conv[l], PAST_LEN, *wl)
        sC.append(C); sn.append(n); sm.append(m); sp.append(pb); sc.append(cb)
    return (yp, ys, jnp.stack(pC), jnp.stack(pn), jnp.stack(pm), jnp.stack(pp), jnp.stack(pc),
            jnp.stack(sC), jnp.stack(sn), jnp.stack(sm), jnp.stack(sp), jnp.stack(sc))
```

```python
import functools

import jax
import jax.numpy as jnp
from jax import lax
from jax.experimental import pallas as pl
from jax.experimental.pallas import tpu as pltpu

F32 = jnp.float32
BF16 = jnp.bfloat16

POOL_WINDOWS = (2, 4, 8, 16)
PAST_LEN = 16384
LN_EPS = 1e-5
PROMPT_CHUNK = 128

LANES = 128
BF16_SUBLANES = 16
VMEM_LIMIT = 56 * 1024 * 1024
POOL_HALO = 16
CONV_HALO = 8
SAMPLE_SEQS = 8


def _params(n_axes):
    return pltpu.CompilerParams(dimension_semantics=("arbitrary",) * n_axes,
                                vmem_limit_bytes=VMEM_LIMIT)


def _tile(dim, target, mult):
    best = None
    for d in range(mult, min(dim, target) + 1, mult):
        if dim % d == 0:
            best = d
    return dim if best is None else best


def _log_sigmoid(x):
    return jnp.minimum(x, 0.0) - jnp.log(1.0 + jnp.exp(-jnp.abs(x)))


def _mm_kernel(*refs, has_bias, has_res, res_scale):
    x_ref, w_ref = refs[0], refs[1]
    o_ref = refs[-1]
    acc = jnp.dot(x_ref[...], w_ref[...], preferred_element_type=F32)
    pos = 2
    if has_bias:
        acc = acc + refs[pos][...]
        pos += 1
    if has_res:
        acc = acc + res_scale * refs[pos][...]
    o_ref[...] = acc.astype(o_ref.dtype)


def _matmul(x, w, bias=None, res=None, res_scale=1.0, out_dtype=F32, tm=1088, tn=512,
            k_blocks=1, k_index=0, name="matmul"):
    t = x.shape[0]
    n = w.shape[1]
    k = x.shape[1] // k_blocks
    tm = _tile(t, tm, BF16_SUBLANES)
    tn = _tile(n, tn, LANES)
    in_specs = [pl.BlockSpec((tm, k), lambda i, j: (i, k_index)),
                pl.BlockSpec((k, tn), lambda i, j: (k_index, j))]
    args = [x, w]
    if bias is not None:
        in_specs.append(pl.BlockSpec((1, tn), lambda i, j: (0, j)))
        args.append(bias)
    if res is not None:
        in_specs.append(pl.BlockSpec((tm, tn), lambda i, j: (i, j)))
        args.append(res)
    return pl.pallas_call(
        functools.partial(_mm_kernel, has_bias=bias is not None, has_res=res is not None,
                          res_scale=res_scale),
        grid=(t // tm, n // tn),
        in_specs=in_specs,
        out_specs=pl.BlockSpec((tm, tn), lambda i, j: (i, j)),
        out_shape=jax.ShapeDtypeStruct((t, n), out_dtype),
        compiler_params=_params(2),
        name=name,
    )(*args)


def _ln_kernel(r_ref, g_ref, b_ref, of_ref, ob_ref):
    r = r_ref[...]
    mu = jnp.mean(r, axis=1, keepdims=True)
    d = r - mu
    var = jnp.mean(d * d, axis=1, keepdims=True)
    y = d * lax.rsqrt(var + LN_EPS) * g_ref[...] + b_ref[...]
    of_ref[...] = y
    ob_ref[...] = y.astype(BF16)


def _layernorm(r, g, b):
    t, d = r.shape
    tr = _tile(t, 272, BF16_SUBLANES)
    row = pl.BlockSpec((tr, d), lambda i: (i, 0))
    vec = pl.BlockSpec((1, d), lambda i: (0, 0))
    return pl.pallas_call(
        _ln_kernel,
        grid=(t // tr,),
        in_specs=[row, vec, vec],
        out_specs=[row, row],
        out_shape=[jax.ShapeDtypeStruct((t, d), F32), jax.ShapeDtypeStruct((t, d), BF16)],
        compiler_params=_params(1),
        name="layernorm",
    )(r, g, b)


def _ffn_up_kernel(x_ref, wg_ref, wu_ref, o_ref):
    x = x_ref[...]
    a = jnp.dot(x, wg_ref[...], preferred_element_type=F32)
    u = jnp.dot(x, wu_ref[...], preferred_element_type=F32)
    o_ref[...] = (a * jax.nn.sigmoid(a) * u).astype(o_ref.dtype)


def _ffn_up(x, wg, wu, tm=1088, tn=512):
    t, k = x.shape
    n = wg.shape[1]
    tm = _tile(t, tm, BF16_SUBLANES)
    tn = _tile(n, tn, LANES)
    wspec = pl.BlockSpec((k, tn), lambda i, j: (0, j))
    return pl.pallas_call(
        _ffn_up_kernel,
        grid=(t // tm, n // tn),
        in_specs=[pl.BlockSpec((tm, k), lambda i, j: (i, 0)), wspec, wspec],
        out_specs=pl.BlockSpec((tm, tn), lambda i, j: (i, j)),
        out_shape=jax.ShapeDtypeStruct((t, n), BF16),
        compiler_params=_params(2),
        name="ffn_up",
    )(x, wg, wu)


def _merge_kernel(hm_ref, zp_ref, zs_ref, wm_ref, wp_ref, ws_ref, gm_ref, gp_ref, gs_ref, o_ref):
    ym = jnp.dot(hm_ref[...], wm_ref[...], preferred_element_type=F32)
    yp = jnp.dot(zp_ref[...], wp_ref[...], preferred_element_type=F32)
    ys = jnp.dot(zs_ref[...], ws_ref[...], preferred_element_type=F32)
    mixed = (jax.nn.sigmoid(gm_ref[...]) * ym + jax.nn.sigmoid(gp_ref[...]) * yp
             + jax.nn.sigmoid(gs_ref[...]) * ys)
    o_ref[...] = mixed.astype(o_ref.dtype)


def _merge(hm, zp, zs, wm, wp, ws, tail, gate_col0, tm=544, tn=512):
    t = hm.shape[0]
    d = wm.shape[1]
    tm = _tile(t, tm, BF16_SUBLANES)
    tn = _tile(d, tn, LANES)
    g0 = gate_col0 // tn
    gd = d // tn

    def xspec(a):
        return pl.BlockSpec((tm, a.shape[1]), lambda i, j: (i, 0))

    def wspec(a):
        return pl.BlockSpec((a.shape[0], tn), lambda i, j: (0, j))

    def gspec(branch):
        return pl.BlockSpec((tm, tn), lambda i, j: (i, g0 + branch * gd + j))

    return pl.pallas_call(
        _merge_kernel,
        grid=(t // tm, d // tn),
        in_specs=[xspec(hm), xspec(zp), xspec(zs), wspec(wm), wspec(wp), wspec(ws),
                  gspec(0), gspec(1), gspec(2)],
        out_specs=pl.BlockSpec((tm, tn), lambda i, j: (i, j)),
        out_shape=jax.ShapeDtypeStruct((t, d), BF16),
        compiler_params=_params(2),
        name="merge",
    )(hm, zp, zs, wm, wp, ws, tail, tail, tail)


def _pick_lane(x, lane_index):
    lane = lax.broadcasted_iota(jnp.int32, x.shape, 1)
    return jnp.sum(jnp.where(lane == lane_index, x, 0.0), axis=1, keepdims=True)


def _col_to_row(col, eye):
    return jnp.sum(jnp.where(eye, col, 0.0), axis=0, keepdims=True)


def _segment_cumsum(x, seg_pos, seg_len):
    s = 1
    while s < seg_len:
        x = x + jnp.where(seg_pos >= s, pltpu.roll(x, s, axis=0), 0.0)
        s *= 2
    return x


def _head_norm_gate(hh, o_pre, mh_g):
    mu = jnp.mean(hh, axis=1, keepdims=True)
    d = hh - mu
    var = jnp.mean(d * d, axis=1, keepdims=True)
    return (jax.nn.sigmoid(o_pre) * (d * lax.rsqrt(var + LN_EPS) * mh_g)).astype(BF16)


def _mlstm_prompt_kernel(q_ref, k_ref, v_ref, g_ref, o_ref, mhg_ref,
                         hm_ref, c_out, n_out, m_out,
                         ct_s, n_s, m_s, *, n_heads, scale):
    h = pl.program_id(1)
    c = pl.program_id(2)
    last = pl.num_programs(2) - 1
    L = q_ref.shape[0]

    @pl.when(c == 0)
    def _():
        ct_s[...] = jnp.zeros_like(ct_s)
        n_s[...] = jnp.zeros_like(n_s)
        m_s[...] = jnp.zeros_like(m_s)

    g = g_ref[...]
    row = lax.broadcasted_iota(jnp.int32, g.shape, 0)
    bsum = _segment_cumsum(_log_sigmoid(g), row, L)
    logi_col = _pick_lane(g, h)
    b_col = _pick_lane(bsum, h + n_heads)
    r2 = lax.broadcasted_iota(jnp.int32, (L, L), 0)
    c2 = lax.broadcasted_iota(jnp.int32, (L, L), 1)
    eye = r2 == c2
    b_row = _col_to_row(b_col, eye)
    logi_row = _col_to_row(logi_col, eye)

    m_prev = m_s[...]
    dmat = jnp.where(r2 >= c2, b_col - b_row + logi_row, -jnp.inf)
    inter = b_col + m_prev
    m_t = jnp.maximum(inter, jnp.max(dmat, axis=1, keepdims=True))
    w_inter = jnp.exp(inter - m_t)

    q = q_ref[...]
    k = k_ref[...]
    v = v_ref[...]
    ct = ct_s[...]
    n_prev = n_s[...]
    qk = lax.dot_general(q, k, (((1,), (1,)), ((), ())), preferred_element_type=F32) * scale
    s = qk * jnp.exp(dmat - m_t)
    num = (w_inter * (jnp.dot(q, ct.astype(BF16), preferred_element_type=F32) * scale)
           + jnp.dot(s.astype(BF16), v, preferred_element_type=F32))
    qn = jnp.sum(q.astype(F32) * n_prev, axis=1, keepdims=True) * scale
    den = w_inter * qn + jnp.sum(s, axis=1, keepdims=True)
    hh = num / jnp.maximum(jnp.abs(den), jnp.exp(-m_t))
    hm_ref[...] = _head_norm_gate(hh, o_ref[...], mhg_ref[...])

    m_new = m_t[L - 1:L, :]
    w_c = jnp.exp(inter[L - 1:L, :] - m_new)
    w_s = jnp.exp(b_col[L - 1:L, :] - b_col + logi_col - m_new)
    vs = (w_s * v.astype(F32)).astype(BF16)
    ct_new = w_c * ct + lax.dot_general(k, vs, (((0,), (0,)), ((), ())),
                                        preferred_element_type=F32)
    n_new = w_c * n_prev + jnp.sum(w_s * k.astype(F32), axis=0, keepdims=True)
    ct_s[...] = ct_new
    n_s[...] = n_new
    m_s[...] = m_new

    @pl.when(c == last)
    def _():
        c_out[0, 0] = ct_new.T
        n_out[0, 0] = n_new
        m_out[0, 0] = m_new


def _mlstm_prompt(qkv, og, gates, mh_g, batch, seq, n_heads, dqk, dv):
    L = PROMPT_CHUNK if seq % PROMPT_CHUNK == 0 else seq
    nc = seq // L
    tp = batch * seq
    kq = 2 * n_heads * dqk // dv

    def rows(b, h, c):
        return b * nc + c

    return pl.pallas_call(
        functools.partial(_mlstm_prompt_kernel, n_heads=n_heads, scale=dqk ** -0.5),
        grid=(batch, n_heads, nc),
        in_specs=[
            pl.BlockSpec((L, dqk), lambda b, h, c: (rows(b, h, c), h)),
            pl.BlockSpec((L, dqk), lambda b, h, c: (rows(b, h, c), n_heads + h)),
            pl.BlockSpec((L, dv), lambda b, h, c: (rows(b, h, c), kq + h)),
            pl.BlockSpec((L, LANES), lambda b, h, c: (rows(b, h, c), 0)),
            pl.BlockSpec((L, dv), lambda b, h, c: (rows(b, h, c), h)),
            pl.BlockSpec((1, dv), lambda b, h, c: (0, h)),
        ],
        out_specs=[
            pl.BlockSpec((L, dv), lambda b, h, c: (rows(b, h, c), h)),
            pl.BlockSpec((1, 1, dv, dqk), lambda b, h, c: (b, h, 0, 0)),
            pl.BlockSpec((1, 1, 1, dqk), lambda b, h, c: (b, h, 0, 0)),
            pl.BlockSpec((1, 1, 1, 1), lambda b, h, c: (b, h, 0, 0)),
        ],
        out_shape=[
            jax.ShapeDtypeStruct((tp, n_heads * dv), BF16),
            jax.ShapeDtypeStruct((batch, n_heads, dv, dqk), F32),
            jax.ShapeDtypeStruct((batch, n_heads, 1, dqk), F32),
            jax.ShapeDtypeStruct((batch, n_heads, 1, 1), F32),
        ],
        scratch_shapes=[pltpu.VMEM((dqk, dv), F32), pltpu.VMEM((1, dqk), F32),
                        pltpu.VMEM((1, 1), F32)],
        compiler_params=_params(3),
        name="mlstm_prompt",
    )(qkv, qkv, qkv, gates, og, mh_g)


def _mlstm_sample_kernel(q_ref, k_ref, v_ref, g_ref, o_ref, mhg_ref, c_ref, n_ref, mrep_ref,
                         hm_ref, c_out, n_out, m_out, *, n_heads, scale, seq_len):
    h = pl.program_id(1)
    R = q_ref.shape[0]
    nseq = R // seq_len

    g = g_ref[...]
    row = lax.broadcasted_iota(jnp.int32, g.shape, 0)
    bsum = _segment_cumsum(_log_sigmoid(g), row % seq_len, seq_len)
    logi_col = _pick_lane(g, h)
    b_col = _pick_lane(bsum, h + n_heads)
    m_col = _pick_lane(mrep_ref[...], h)

    r2 = lax.broadcasted_iota(jnp.int32, (R, R), 0)
    c2 = lax.broadcasted_iota(jnp.int32, (R, R), 1)
    eye = r2 == c2
    same = (r2 // seq_len) == (c2 // seq_len)
    b_row = _col_to_row(b_col, eye)
    logi_row = _col_to_row(logi_col, eye)
    dmat = jnp.where(same & (c2 <= r2), b_col - b_row + logi_row, -jnp.inf)
    inter = b_col + m_col
    m_t = jnp.maximum(inter, jnp.max(dmat, axis=1, keepdims=True))
    w_inter = jnp.exp(inter - m_t)

    q = q_ref[...]
    k = k_ref[...]
    v = v_ref[...]
    q32 = q.astype(F32)
    qk = lax.dot_general(q, k, (((1,), (1,)), ((), ())), preferred_element_type=F32) * scale
    s = qk * jnp.exp(dmat - m_t)
    num_s = jnp.dot(s.astype(BF16), v, preferred_element_type=F32)

    seq_of_row = lax.broadcasted_iota(jnp.int32, (R, 1), 0) // seq_len
    num_c = jnp.zeros(num_s.shape, F32)
    qn = jnp.zeros((R, 1), F32)
    for j in range(nseq):
        mine = seq_of_row == j
        cq = lax.dot_general(q, c_ref[j, 0].astype(BF16), (((1,), (1,)), ((), ())),
                             preferred_element_type=F32)
        num_c = jnp.where(mine, cq, num_c)
        qn = jnp.where(mine, jnp.sum(q32 * n_ref[j, 0], axis=1, keepdims=True), qn)
    num = w_inter * (num_c * scale) + num_s
    den = w_inter * (qn * scale) + jnp.sum(s, axis=1, keepdims=True)
    hh = num / jnp.maximum(jnp.abs(den), jnp.exp(-m_t))
    hm_ref[...] = _head_norm_gate(hh, o_ref[...], mhg_ref[...])

    pick_last = same & (c2 % seq_len == seq_len - 1)

    def last_of_seq(col):
        return jnp.sum(jnp.where(pick_last, _col_to_row(col, eye), 0.0), axis=1, keepdims=True)

    m_new = last_of_seq(m_t)
    w_c = jnp.exp(last_of_seq(inter) - m_new)
    w_s = jnp.exp(last_of_seq(b_col) - b_col + logi_col - m_new)
    vs = (w_s * v.astype(F32)).astype(BF16)
    kw = w_s * k.astype(F32)
    for j in range(nseq):
        mine = seq_of_row == j
        upd = lax.dot_general(jnp.where(mine, vs, jnp.zeros_like(vs)), k,
                              (((0,), (0,)), ((), ())), preferred_element_type=F32)
        wc_j = w_c[j * seq_len:j * seq_len + 1, :]
        c_out[j, 0] = wc_j * c_ref[j, 0] + upd
        n_out[j, 0] = wc_j * n_ref[j, 0] + jnp.sum(jnp.where(mine, kw, 0.0), axis=0, keepdims=True)
    m_out[0] = m_new


def _mlstm_sample(qkv, og, gates, mh_g, state_c, state_n, m_rep, row0, n_seq, seq_len,
                  n_heads, dqk, dv):
    nb = _tile(n_seq, SAMPLE_SEQS, 1)
    R = nb * seq_len
    blk0 = row0 // R
    kq = 2 * n_heads * dqk // dv
    ts = n_seq * seq_len

    return pl.pallas_call(
        functools.partial(_mlstm_sample_kernel, n_heads=n_heads, scale=dqk ** -0.5,
                          seq_len=seq_len),
        grid=(n_seq // nb, n_heads),
        in_specs=[
            pl.BlockSpec((R, dqk), lambda i, h: (blk0 + i, h)),
            pl.BlockSpec((R, dqk), lambda i, h: (blk0 + i, n_heads + h)),
            pl.BlockSpec((R, dv), lambda i, h: (blk0 + i, kq + h)),
            pl.BlockSpec((R, LANES), lambda i, h: (blk0 + i, 0)),
            pl.BlockSpec((R, dv), lambda i, h: (blk0 + i, h)),
            pl.BlockSpec((1, dv), lambda i, h: (0, h)),
            pl.BlockSpec((nb, 1, dv, dqk), lambda i, h: (i, h, 0, 0)),
            pl.BlockSpec((nb, 1, 1, dqk), lambda i, h: (i, h, 0, 0)),
            pl.BlockSpec((R, n_heads), lambda i, h: (i, 0)),
        ],
        out_specs=[
            pl.BlockSpec((R, dv), lambda i, h: (i, h)),
            pl.BlockSpec((nb, 1, dv, dqk), lambda i, h: (i, h, 0, 0)),
            pl.BlockSpec((nb, 1, 1, dqk), lambda i, h: (i, h, 0, 0)),
            pl.BlockSpec((1, R, 1), lambda i, h: (h, i, 0)),
        ],
        out_shape=[
            jax.ShapeDtypeStruct((ts, n_heads * dv), BF16),
            jax.ShapeDtypeStruct(state_c.shape, F32),
            jax.ShapeDtypeStruct(state_n.shape, F32),
            jax.ShapeDtypeStruct((n_heads, ts, 1), F32),
        ],
        compiler_params=_params(2),
        name="mlstm_sample",
    )(qkv, qkv, qkv, gates, og, mh_g, state_c, state_n, m_rep)


def _pool_conv_prompt_kernel(u_ref, uh_ref, cb_ref, cc_ref, cx_ref, cch_ref, cxh_ref,
                             maps_ref, scale_ref, cw_ref,
                             zp_ref, zs_ref, ps_ref, cs_ref, full_s, pf_s, *, windows):
    t = pl.program_id(1)
    last = pl.num_programs(1) - 1
    tr = u_ref.shape[0]
    gw = maps_ref.shape[1]
    n_buf = ps_ref.shape[1]
    taps = cw_ref.shape[0]
    has_past = t > 0

    u = u_ref[...]
    full_s[0:POOL_HALO, :] = jnp.where(has_past, uh_ref[...], 0.0)
    full_s[POOL_HALO:, :] = u
    pos = t * tr + lax.broadcasted_iota(jnp.int32, (tr, 1), 0)
    for g, w in enumerate(windows):
        cols = slice(g * gw, (g + 1) * gw)
        acc = u[:, cols]
        for j in range(1, w):
            acc = acc + full_s[POOL_HALO - j:POOL_HALO - j + tr, cols]
        cnt = jnp.minimum(pos + 1, w).astype(F32)
        zf = acc / cnt - u[:, cols]
        zp = jnp.dot(zf.astype(BF16), maps_ref[g], preferred_element_type=F32) * scale_ref[:, cols]
        zp_ref[:, cols] = zp.astype(BF16)

    p = cc_ref[...] * cx_ref[...]
    pf_s[0:CONV_HALO, :] = jnp.where(has_past, cch_ref[...] * cxh_ref[...], 0.0)
    pf_s[CONV_HALO:, :] = p
    y = p * cw_ref[taps - 1:taps, :]
    for j in range(taps - 1):
        back = taps - 1 - j
        y = y + pf_s[CONV_HALO - back:CONV_HALO - back + tr, :] * cw_ref[j:j + 1, :]
    zs_ref[...] = (cb_ref[...] * y).astype(BF16)

    @pl.when(t == last)
    def _():
        ps_ref[0] = full_s[POOL_HALO + tr - n_buf:POOL_HALO + tr, :]
        cs_ref[0] = pf_s[CONV_HALO + tr - (taps - 1):CONV_HALO + tr, :]


def _pool_conv_prompt(tail, maps, pscale, conv_w, batch, seq, width, n_buf):
    tr = _tile(seq, 256, POOL_HALO)
    nt = seq // tr
    taps = conv_w.shape[0]
    hp = tr // POOL_HALO
    hc = tr // CONV_HALO

    def rows(b, t):
        return b * nt + t

    def main(col):
        return pl.BlockSpec((tr, width), lambda b, t: (rows(b, t), col))

    def halo(col, h, per_tile):
        return pl.BlockSpec((h, width), lambda b, t: (jnp.maximum(rows(b, t) * per_tile - 1, 0), col))

    return pl.pallas_call(
        functools.partial(_pool_conv_prompt_kernel, windows=POOL_WINDOWS),
        grid=(batch, nt),
        in_specs=[main(0), halo(0, POOL_HALO, hp), main(1), main(2), main(3),
                  halo(2, CONV_HALO, hc), halo(3, CONV_HALO, hc),
                  pl.BlockSpec(maps.shape, lambda b, t: (0, 0, 0)),
                  pl.BlockSpec((1, width), lambda b, t: (0, 0)),
                  pl.BlockSpec((taps, width), lambda b, t: (0, 0))],
        out_specs=[pl.BlockSpec((tr, width), lambda b, t: (rows(b, t), 0)),
                   pl.BlockSpec((tr, width), lambda b, t: (rows(b, t), 0)),
                   pl.BlockSpec((1, n_buf, width), lambda b, t: (b, 0, 0)),
                   pl.BlockSpec((1, taps - 1, width), lambda b, t: (b, 0, 0))],
        out_shape=[jax.ShapeDtypeStruct((batch * seq, width), BF16),
                   jax.ShapeDtypeStruct((batch * seq, width), BF16),
                   jax.ShapeDtypeStruct((batch, n_buf, width), F32),
                   jax.ShapeDtypeStruct((batch, taps - 1, width), F32)],
        scratch_shapes=[pltpu.VMEM((POOL_HALO + tr, width), F32),
                        pltpu.VMEM((CONV_HALO + tr, width), F32)],
        compiler_params=_params(2),
        name="pool_conv_prompt",
    )(tail, tail, tail, tail, tail, tail, tail, maps, pscale, conv_w)


def _pool_conv_sample_kernel(x_ref, ps_ref, cs_ref, maps_ref, scale_ref, cw_ref,
                             zp_ref, zs_ref, pn_ref, cn_ref, *, windows, start_pos):
    g = pl.program_id(0)
    seq_len = x_ref.shape[1]
    n_buf = ps_ref.shape[0]
    taps = cw_ref.shape[0]
    win = jnp.int32(windows[0])
    for i, w in enumerate(windows):
        win = jnp.where(g == i, jnp.int32(w), win)

    hist = [ps_ref[j] for j in range(n_buf)] + [x_ref[0, t] for t in range(seq_len)]
    for t in range(seq_len):
        cur = n_buf + t
        acc = hist[cur]
        for j in range(1, max(windows)):
            acc = acc + jnp.where(j < win, hist[cur - j], 0.0)
        cnt = jnp.minimum(start_pos + t + 1, win).astype(F32)
        zf = acc / cnt - hist[cur]
        zp = jnp.dot(zf.astype(BF16), maps_ref[0], preferred_element_type=F32) * scale_ref[...]
        zp_ref[t] = zp.astype(BF16)
    for j in range(n_buf):
        pn_ref[j] = hist[seq_len + j]

    prod = [cs_ref[j] for j in range(taps - 1)] + [x_ref[2, t] * x_ref[3, t] for t in range(seq_len)]
    for t in range(seq_len):
        y = prod[t] * cw_ref[0:1, :]
        for j in range(1, taps):
            y = y + prod[t + j] * cw_ref[j:j + 1, :]
        zs_ref[t] = (x_ref[1, t] * y).astype(BF16)
    for j in range(taps - 1):
        cn_ref[j] = prod[seq_len + j]


def _pool_conv_sample(x4, pstate_t, cstate_t, maps, pscale, conv_w):
    _, seq_len, n_seq, width = x4.shape
    n_buf = pstate_t.shape[0]
    taps = conv_w.shape[0]
    n_groups, gw = maps.shape[0], maps.shape[1]
    assert len(POOL_WINDOWS) == n_groups and n_buf >= max(POOL_WINDOWS) - 1

    return pl.pallas_call(
        functools.partial(_pool_conv_sample_kernel, windows=POOL_WINDOWS, start_pos=PAST_LEN),
        grid=(n_groups,),
        in_specs=[pl.BlockSpec((4, seq_len, n_seq, gw), lambda g: (0, 0, 0, g)),
                  pl.BlockSpec((n_buf, n_seq, gw), lambda g: (0, 0, g)),
                  pl.BlockSpec((taps - 1, n_seq, gw), lambda g: (0, 0, g)),
                  pl.BlockSpec((1, gw, gw), lambda g: (g, 0, 0)),
                  pl.BlockSpec((1, gw), lambda g: (0, g)),
                  pl.BlockSpec((taps, gw), lambda g: (0, g))],
        out_specs=[pl.BlockSpec((seq_len, n_seq, gw), lambda g: (0, 0, g)),
                   pl.BlockSpec((seq_len, n_seq, gw), lambda g: (0, 0, g)),
                   pl.BlockSpec((n_buf, n_seq, gw), lambda g: (0, 0, g)),
                   pl.BlockSpec((taps - 1, n_seq, gw), lambda g: (0, 0, g))],
        out_shape=[jax.ShapeDtypeStruct((seq_len, n_seq, width), BF16),
                   jax.ShapeDtypeStruct((seq_len, n_seq, width), BF16),
                   jax.ShapeDtypeStruct(pstate_t.shape, F32),
                   jax.ShapeDtypeStruct(cstate_t.shape, F32)],
        compiler_params=_params(1),
        name="pool_conv_sample",
    )(x4, pstate_t, cstate_t, maps, pscale, conv_w)


def _pad_cols(w, n):
    return jnp.pad(w, ((0, 0), (0, n - w.shape[1])))


def kernel(x_prompt, x_sample, state_C, state_n, state_m, state_pool, state_conv, w_in, b_in, mh_g, pool_maps, pool_scale, conv_w, w_bm, w_bp, w_bs, w_o, ln1_g, ln1_b, w_gate, w_up, w_down, ln2_g, ln2_b):
    depth = w_in.shape[0]
    batch, seq, d = x_prompt.shape
    n_seq, seq_len, _ = x_sample.shape
    n_heads, dv, dqk = state_C.shape[2], state_C.shape[3], state_C.shape[4]
    n_buf, pw = state_pool.shape[2], state_pool.shape[3]
    taps, cw = conv_w.shape[1], conv_w.shape[2]
    dff = w_gate.shape[2]
    assert pw == cw and dv == 2 * dqk and 2 * n_heads <= LANES
    alpha = (2 * depth) ** 0.25
    tp, ts = batch * seq, n_seq * seq_len

    off_o = 2 * n_heads * dqk + n_heads * dv
    off_i = off_o + n_heads * dv
    off_p = off_i + 2 * n_heads
    gate_col0 = pw + 3 * cw
    dff_pad = -(-dff // 1024) * 1024 if dff > 1024 else dff

    x = jnp.concatenate([x_prompt.reshape(tp, d), x_sample.reshape(ts, d)], axis=0)
    xb = x.astype(BF16)

    outs = [[] for _ in range(10)]
    for l in range(depth):
        wl = w_in[l]
        bl = b_in[l][None, :]
        w_qkv = wl[:, :off_o].astype(BF16)
        w_og = wl[:, off_o:off_i].astype(BF16)
        w_if = _pad_cols(wl[:, off_i:off_p], LANES).astype(BF16)
        w_tail = wl[:, off_p:].astype(BF16)
        b_if = _pad_cols(bl[:, off_i:off_p], LANES)

        qkv = _matmul(xb, w_qkv, bias=bl[:, :off_o], out_dtype=BF16, tn=1024, name="in_qkv")
        og = _matmul(xb, w_og, bias=bl[:, off_o:off_i], tn=1024, name="in_ogate")
        gates = _matmul(xb, w_if, bias=b_if, name="in_if")
        tail = _matmul(xb, w_tail, bias=bl[:, off_p:], tn=1024, name="in_tail")

        mhg = mh_g[l][None, :]
        hm_p, c_p, n_p, m_p = _mlstm_prompt(qkv, og, gates, mhg, batch, seq, n_heads, dqk, dv)
        m_rep = jnp.repeat(state_m[l], seq_len, axis=0)
        hm_s, c_s, n_s, m_s = _mlstm_sample(
            qkv, og, gates, mhg, state_C[l], state_n[l][:, :, None, :], m_rep,
            tp, n_seq, seq_len, n_heads, dqk, dv)
        hm = jnp.concatenate([hm_p, hm_s], axis=0)

        maps = pool_maps[l].astype(BF16)
        pscale = pool_scale[l][None, :]
        zp_p, zs_p, ps_p, cs_p = _pool_conv_prompt(tail, maps, pscale, conv_w[l], batch, seq, pw, n_buf)
        x4 = tail[tp:, :pw + 3 * cw].reshape(n_seq, seq_len, 4, pw).transpose(2, 1, 0, 3)
        zp_s, zs_s, ps_s, cs_s = _pool_conv_sample(
            x4, state_pool[l].transpose(1, 0, 2), state_conv[l].transpose(1, 0, 2),
            maps, pscale, conv_w[l])
        zp = jnp.concatenate([zp_p, zp_s.transpose(1, 0, 2).reshape(ts, pw)], axis=0)
        zs = jnp.concatenate([zs_p, zs_s.transpose(1, 0, 2).reshape(ts, cw)], axis=0)

        mixed = _merge(hm, zp, zs, w_bm[l].astype(BF16), w_bp[l].astype(BF16),
                       w_bs[l].astype(BF16), tail, gate_col0)
        r1 = _matmul(mixed, w_o[l].astype(BF16), res=x, res_scale=alpha, name="out_proj")
        x1, x1b = _layernorm(r1, ln1_g[l][None, :], ln1_b[l][None, :])

        ff = _ffn_up(x1b, _pad_cols(w_gate[l], dff_pad).astype(BF16),
                     _pad_cols(w_up[l], dff_pad).astype(BF16))
        wd = jnp.pad(w_down[l], ((0, dff_pad - dff), (0, 0))).astype(BF16)
        k_blocks = 2 if dff_pad % (2 * LANES) == 0 else 1
        r2 = _matmul(ff, wd, res=x1, res_scale=alpha, k_blocks=k_blocks, name="ffn_down")
        for kb in range(1, k_blocks):
            r2 = _matmul(ff, wd, res=r2, k_blocks=k_blocks, k_index=kb, name="ffn_down")
        x, xb = _layernorm(r2, ln2_g[l][None, :], ln2_b[l][None, :])

        new = (c_p, n_p[:, :, 0, :], m_p[:, :, 0, 0], ps_p, cs_p,
               c_s, n_s[:, :, 0, :], m_s[:, seq_len - 1::seq_len, 0].T,
               ps_s.transpose(1, 0, 2), cs_s.transpose(1, 0, 2))
        for acc, val in zip(outs, new):
            acc.append(val)

    return (x[:tp].reshape(batch, seq, d), x[tp:].reshape(n_seq, seq_len, d)) + tuple(
        jnp.stack(o) for o in outs)
```

```python
import functools

import jax
import jax.numpy as jnp
from jax import lax
from jax.experimental import pallas as pl
from jax.experimental.pallas import tpu as pltpu

F32 = jnp.float32
BF16 = jnp.bfloat16

POOL_WINDOWS = (2, 4, 8, 16)
PAST_LEN = 16384
LN_EPS = 1e-5
PROMPT_CHUNK = 128

LANES = 128
BF16_SUBLANES = 16
VMEM_LIMIT = 56 * 1024 * 1024
POOL_HALO = 16
CONV_HALO = 8
SAMPLE_SEQS = 8
FF_ALIGN = 1024


def _params(n_axes):
    return pltpu.CompilerParams(dimension_semantics=("arbitrary",) * n_axes,
                                vmem_limit_bytes=VMEM_LIMIT)


def _tile(dim, target, mult):
    best = None
    for d in range(mult, min(dim, target) + 1, mult):
        if dim % d == 0:
            best = d
    return dim if best is None else best


def _log_sigmoid(x):
    return jnp.minimum(x, 0.0) - jnp.log(1.0 + jnp.exp(-jnp.abs(x)))


def _cast_kernel(x_ref, o_ref, *, src_row_blocks, src_col_blocks):
    valid = (pl.program_id(1) < src_row_blocks) & (pl.program_id(2) < src_col_blocks)
    o_ref[...] = jnp.where(valid, x_ref[...], 0.0).astype(o_ref.dtype)


def _to_bf16(w, col0=0, ncols=None, rows_out=None, cols_out=None):
    layers, rows, cols = w.shape
    ncols = cols - col0 if ncols is None else ncols
    rows_out = rows if rows_out is None else rows_out
    cols_out = ncols if cols_out is None else cols_out
    tc = LANES
    for cand in range(LANES, 4096 + 1, LANES):
        if col0 % cand == 0 and ncols % cand == 0 and cols_out % cand == 0:
            tc = cand
    tr = BF16_SUBLANES
    for cand in range(BF16_SUBLANES, rows + 1, BF16_SUBLANES):
        if rows % cand == 0 and rows_out % cand == 0 and cand * tc <= 1024 * 1024:
            tr = cand
    src_r, src_c = rows // tr, ncols // tc
    return pl.pallas_call(
        functools.partial(_cast_kernel, src_row_blocks=src_r, src_col_blocks=src_c),
        grid=(layers, rows_out // tr, cols_out // tc),
        in_specs=[pl.BlockSpec((None, tr, tc), lambda l, i, j: (
            l, jnp.minimum(i, src_r - 1), col0 // tc + jnp.minimum(j, src_c - 1)))],
        out_specs=pl.BlockSpec((None, tr, tc), lambda l, i, j: (l, i, j)),
        out_shape=jax.ShapeDtypeStruct((layers, rows_out, cols_out), BF16),
        compiler_params=_params(3),
        name="to_bf16",
    )(w)


def _cast_transposed_kernel(a_ref, *rest, shift):
    o_ref = rest[-1]
    a = a_ref[...]
    if shift:
        a = jnp.concatenate([a[shift:], rest[0][...]], axis=0)
    o_ref[...] = a.T.astype(o_ref.dtype)


def _to_bf16_transposed(wt, row0, nrows, tn=512):
    layers, _, k = wt.shape
    shift = row0 % LANES
    base = row0 - shift
    tn = _tile(nrows, tn, LANES)
    assert base % tn == 0 and shift % 8 == 0
    in_specs = [pl.BlockSpec((None, tn, k), lambda l, j: (l, base // tn + j, 0))]
    args = [wt]
    if shift:
        in_specs.append(pl.BlockSpec((None, shift, k),
                                     lambda l, j: (l, (base + (j + 1) * tn) // shift, 0)))
        args.append(wt)
    return pl.pallas_call(
        functools.partial(_cast_transposed_kernel, shift=shift),
        grid=(layers, nrows // tn),
        in_specs=in_specs,
        out_specs=pl.BlockSpec((None, k, tn), lambda l, j: (l, 0, j)),
        out_shape=jax.ShapeDtypeStruct((layers, k, nrows), BF16),
        compiler_params=_params(2),
        name="to_bf16_transposed",
    )(*args)


def _mm_kernel(*refs, has_bias, has_res, res_scale):
    x_ref, w_ref = refs[0], refs[1]
    o_ref = refs[-1]
    acc = jnp.dot(x_ref[...], w_ref[...], preferred_element_type=F32)
    pos = 2
    if has_bias:
        acc = acc + refs[pos][...]
        pos += 1
    if has_res:
        acc = acc + res_scale * refs[pos][...]
    o_ref[...] = acc.astype(o_ref.dtype)


def _matmul(x, w, layer, bias=None, res=None, res_scale=1.0, out_dtype=F32, tm=1088, tn=512,
            k_blocks=1, k_index=0, name="matmul"):
    t = x.shape[0]
    n = w.shape[2]
    k = x.shape[1] // k_blocks
    tm = _tile(t, tm, BF16_SUBLANES)
    tn = _tile(n, tn, LANES)
    in_specs = [pl.BlockSpec((tm, k), lambda i, j: (i, k_index)),
                pl.BlockSpec((None, k, tn), lambda i, j: (layer, k_index, j))]
    args = [x, w]
    if bias is not None:
        in_specs.append(pl.BlockSpec((None, 1, tn), lambda i, j: (layer, 0, j)))
        args.append(bias)
    if res is not None:
        in_specs.append(pl.BlockSpec((tm, tn), lambda i, j: (i, j)))
        args.append(res)
    return pl.pallas_call(
        functools.partial(_mm_kernel, has_bias=bias is not None, has_res=res is not None,
                          res_scale=res_scale),
        grid=(t // tm, n // tn),
        in_specs=in_specs,
        out_specs=pl.BlockSpec((tm, tn), lambda i, j: (i, j)),
        out_shape=jax.ShapeDtypeStruct((t, n), out_dtype),
        compiler_params=_params(2),
        name=name,
    )(*args)


def _ln_kernel(r_ref, g_ref, b_ref, of_ref, *maybe_bf16_ref):
    r = r_ref[...]
    mu = jnp.mean(r, axis=1, keepdims=True)
    d = r - mu
    var = jnp.mean(d * d, axis=1, keepdims=True)
    y = d * lax.rsqrt(var + LN_EPS) * g_ref[...] + b_ref[...]
    of_ref[...] = y
    for ob_ref in maybe_bf16_ref:
        ob_ref[...] = y.astype(BF16)


def _layernorm(r, g, b, layer, row0=0, n_rows=None, with_bf16=True):
    d = r.shape[1]
    n_rows = r.shape[0] - row0 if n_rows is None else n_rows
    tr = BF16_SUBLANES
    for cand in range(BF16_SUBLANES, 272 + 1, BF16_SUBLANES):
        if row0 % cand == 0 and n_rows % cand == 0:
            tr = cand
    blk0 = row0 // tr
    vec = pl.BlockSpec((None, 1, d), lambda i: (layer, 0, 0))
    out_row = pl.BlockSpec((tr, d), lambda i: (i, 0))
    out_specs = [out_row]
    out_shape = [jax.ShapeDtypeStruct((n_rows, d), F32)]
    if with_bf16:
        out_specs.append(out_row)
        out_shape.append(jax.ShapeDtypeStruct((n_rows, d), BF16))
    return pl.pallas_call(
        _ln_kernel,
        grid=(n_rows // tr,),
        in_specs=[pl.BlockSpec((tr, d), lambda i: (blk0 + i, 0)), vec, vec],
        out_specs=out_specs,
        out_shape=out_shape,
        compiler_params=_params(1),
        name="layernorm",
    )(r, g, b)


def _ffn_up_kernel(x_ref, wg_ref, wu_ref, o_ref):
    x = x_ref[...]
    a = jnp.dot(x, wg_ref[...], preferred_element_type=F32)
    u = jnp.dot(x, wu_ref[...], preferred_element_type=F32)
    o_ref[...] = (a * jax.nn.sigmoid(a) * u).astype(o_ref.dtype)


def _ffn_up(x, wg, wu, layer, tm=1088, tn=512):
    t, k = x.shape
    n = wg.shape[2]
    tm = _tile(t, tm, BF16_SUBLANES)
    tn = _tile(n, tn, LANES)
    wspec = pl.BlockSpec((None, k, tn), lambda i, j: (layer, 0, j))
    return pl.pallas_call(
        _ffn_up_kernel,
        grid=(t // tm, n // tn),
        in_specs=[pl.BlockSpec((tm, k), lambda i, j: (i, 0)), wspec, wspec],
        out_specs=pl.BlockSpec((tm, tn), lambda i, j: (i, j)),
        out_shape=jax.ShapeDtypeStruct((t, n), BF16),
        compiler_params=_params(2),
        name="ffn_up",
    )(x, wg, wu)


def _merge_kernel(hm_ref, zp_ref, zs_ref, wm_ref, wp_ref, ws_ref, gm_ref, gp_ref, gs_ref, o_ref):
    ym = jnp.dot(hm_ref[...], wm_ref[...], preferred_element_type=F32)
    yp = jnp.dot(zp_ref[...], wp_ref[...], preferred_element_type=F32)
    ys = jnp.dot(zs_ref[...], ws_ref[...], preferred_element_type=F32)
    mixed = (jax.nn.sigmoid(gm_ref[...]) * ym + jax.nn.sigmoid(gp_ref[...]) * yp
             + jax.nn.sigmoid(gs_ref[...]) * ys)
    o_ref[...] = mixed.astype(o_ref.dtype)


def _merge(hm, zp, zs, wm, wp, ws, layer, tail, gate_col0, tm=544, tn=512):
    t = hm.shape[0]
    d = wm.shape[2]
    tm = _tile(t, tm, BF16_SUBLANES)
    tn = _tile(d, tn, LANES)
    g0 = gate_col0 // tn
    gd = d // tn

    def xspec(a):
        return pl.BlockSpec((tm, a.shape[1]), lambda i, j: (i, 0))

    def wspec(a):
        return pl.BlockSpec((None, a.shape[1], tn), lambda i, j: (layer, 0, j))

    def gspec(branch):
        return pl.BlockSpec((tm, tn), lambda i, j: (i, g0 + branch * gd + j))

    return pl.pallas_call(
        _merge_kernel,
        grid=(t // tm, d // tn),
        in_specs=[xspec(hm), xspec(zp), xspec(zs), wspec(wm), wspec(wp), wspec(ws),
                  gspec(0), gspec(1), gspec(2)],
        out_specs=pl.BlockSpec((tm, tn), lambda i, j: (i, j)),
        out_shape=jax.ShapeDtypeStruct((t, d), BF16),
        compiler_params=_params(2),
        name="merge",
    )(hm, zp, zs, wm, wp, ws, tail, tail, tail)


def _pick_lane(x, lane_index):
    lane = lax.broadcasted_iota(jnp.int32, x.shape, 1)
    return jnp.sum(jnp.where(lane == lane_index, x, 0.0), axis=1, keepdims=True)


def _col_to_row(col, eye):
    return jnp.sum(jnp.where(eye, col, 0.0), axis=0, keepdims=True)


def _segment_cumsum(x, seg_pos, seg_len):
    s = 1
    while s < seg_len:
        x = x + jnp.where(seg_pos >= s, pltpu.roll(x, s, axis=0), 0.0)
        s *= 2
    return x


def _head_norm_gate(hh, o_pre, mh_g):
    mu = jnp.mean(hh, axis=1, keepdims=True)
    d = hh - mu
    var = jnp.mean(d * d, axis=1, keepdims=True)
    return (jax.nn.sigmoid(o_pre.astype(F32)) * (d * lax.rsqrt(var + LN_EPS) * mh_g)).astype(BF16)


def _mlstm_prompt_kernel(q_ref, k_ref, v_ref, o_ref, g_ref, mhg_ref,
                         hm_ref, c_out, n_out, m_out,
                         ct_s, n_s, m_s, *, n_heads, dqk, dv, scale):
    c = pl.program_id(1)
    last = pl.num_programs(1) - 1
    L = q_ref.shape[0]

    @pl.when(c == 0)
    def _():
        ct_s[...] = jnp.zeros_like(ct_s)
        n_s[...] = jnp.zeros_like(n_s)
        m_s[...] = jnp.zeros_like(m_s)

    g = g_ref[...]
    row = lax.broadcasted_iota(jnp.int32, g.shape, 0)
    lane = lax.broadcasted_iota(jnp.int32, g.shape, 1)
    bsum = _segment_cumsum(_log_sigmoid(g), row, L)
    gb = jnp.where(lane < n_heads, g, bsum)
    gb_t = gb.T
    r2 = lax.broadcasted_iota(jnp.int32, (L, L), 0)
    c2 = lax.broadcasted_iota(jnp.int32, (L, L), 1)
    causal = r2 >= c2

    for h in range(n_heads):
        logi_col = gb[:, h:h + 1]
        b_col = gb[:, n_heads + h:n_heads + h + 1]
        logi_row = gb_t[h:h + 1, :]
        b_row = gb_t[n_heads + h:n_heads + h + 1, :]
        m_prev = m_s[h:h + 1, :]
        dmat = jnp.where(causal, b_col - b_row + logi_row, -jnp.inf)
        inter = b_col + m_prev
        m_t = jnp.maximum(inter, jnp.max(dmat, axis=1, keepdims=True))
        w_inter = jnp.exp(inter - m_t)

        q = q_ref[:, h * dqk:(h + 1) * dqk]
        k = k_ref[:, h * dqk:(h + 1) * dqk]
        v = v_ref[:, h * dv:(h + 1) * dv]
        ct = ct_s[h]
        n_prev = n_s[h:h + 1, :]
        qk = lax.dot_general(q, k, (((1,), (1,)), ((), ())), preferred_element_type=F32) * scale
        s = qk * jnp.exp(dmat - m_t)
        num = (w_inter * (jnp.dot(q, ct.astype(BF16), preferred_element_type=F32) * scale)
               + jnp.dot(s.astype(BF16), v, preferred_element_type=F32))
        qn = jnp.sum(q.astype(F32) * n_prev, axis=1, keepdims=True) * scale
        den = w_inter * qn + jnp.sum(s, axis=1, keepdims=True)
        hh = num / jnp.maximum(jnp.abs(den), jnp.exp(-m_t))
        hm_ref[:, h * dv:(h + 1) * dv] = _head_norm_gate(
            hh, o_ref[:, h * dv:(h + 1) * dv], mhg_ref[:, h * dv:(h + 1) * dv])

        m_new = m_t[L - 1:L, :]
        w_c = jnp.exp(inter[L - 1:L, :] - m_new)
        w_s = jnp.exp(b_col[L - 1:L, :] - b_col + logi_col - m_new)
        vs = (w_s * v.astype(F32)).astype(BF16)
        ct_new = w_c * ct + lax.dot_general(k, vs, (((0,), (0,)), ((), ())),
                                            preferred_element_type=F32)
        n_new = w_c * n_prev + jnp.sum(w_s * k.astype(F32), axis=0, keepdims=True)
        ct_s[h] = ct_new
        n_s[h:h + 1, :] = n_new
        m_s[h:h + 1, :] = m_new

    @pl.when(c == last)
    def _():
        for h in range(n_heads):
            c_out[0, h] = ct_s[h].T
        n_out[0] = n_s[...]
        m_out[0] = m_s[...]


def _mlstm_prompt(qkvo, gates, mh_g, layer, batch, seq, n_rows, n_heads, dqk, dv):
    L = PROMPT_CHUNK if seq % PROMPT_CHUNK == 0 else seq
    nc = seq // L
    qw, vw = n_heads * dqk, n_heads * dv
    assert qw % LANES == 0 and 2 * qw % vw == 0

    def rows(b, c):
        return b * nc + c

    return pl.pallas_call(
        functools.partial(_mlstm_prompt_kernel, n_heads=n_heads, dqk=dqk, dv=dv,
                          scale=dqk ** -0.5),
        grid=(batch, nc),
        in_specs=[
            pl.BlockSpec((L, qw), lambda b, c: (rows(b, c), 0)),
            pl.BlockSpec((L, qw), lambda b, c: (rows(b, c), 1)),
            pl.BlockSpec((L, vw), lambda b, c: (rows(b, c), 2 * qw // vw)),
            pl.BlockSpec((L, vw), lambda b, c: (rows(b, c), 2 * qw // vw + 1)),
            pl.BlockSpec((L, LANES), lambda b, c: (rows(b, c), 0)),
            pl.BlockSpec((None, 1, vw), lambda b, c: (layer, 0, 0)),
        ],
        out_specs=[
            pl.BlockSpec((L, vw), lambda b, c: (rows(b, c), 0)),
            pl.BlockSpec((1, n_heads, dv, dqk), lambda b, c: (b, 0, 0, 0)),
            pl.BlockSpec((1, n_heads, dqk), lambda b, c: (b, 0, 0)),
            pl.BlockSpec((1, n_heads, 1), lambda b, c: (b, 0, 0)),
        ],
        out_shape=[
            jax.ShapeDtypeStruct((n_rows, vw), BF16),
            jax.ShapeDtypeStruct((batch, n_heads, dv, dqk), F32),
            jax.ShapeDtypeStruct((batch, n_heads, dqk), F32),
            jax.ShapeDtypeStruct((batch, n_heads, 1), F32),
        ],
        scratch_shapes=[pltpu.VMEM((n_heads, dqk, dv), F32), pltpu.VMEM((n_heads, dqk), F32),
                        pltpu.VMEM((n_heads, 1), F32)],
        compiler_params=_params(2),
        name="mlstm_prompt",
    )(qkvo, qkvo, qkvo, qkvo, gates, mh_g)


def _mlstm_sample_kernel(q_ref, k_ref, v_ref, o_ref, g_ref, mhg_ref, c_ref, n_ref, mrep_ref,
                         *rest, n_heads, scale, seq_len):
    hm_ref, c_out, n_out, m_out = rest[-4:]
    h = pl.program_id(1)
    R = q_ref.shape[0]
    nseq = R // seq_len

    g = g_ref[...]
    row = lax.broadcasted_iota(jnp.int32, g.shape, 0)
    bsum = _segment_cumsum(_log_sigmoid(g), row % seq_len, seq_len)
    logi_col = _pick_lane(g, h)
    b_col = _pick_lane(bsum, h + n_heads)
    m_col = _pick_lane(mrep_ref[...], h)

    r2 = lax.broadcasted_iota(jnp.int32, (R, R), 0)
    c2 = lax.broadcasted_iota(jnp.int32, (R, R), 1)
    eye = r2 == c2
    same = (r2 // seq_len) == (c2 // seq_len)
    b_row = _col_to_row(b_col, eye)
    logi_row = _col_to_row(logi_col, eye)
    dmat = jnp.where(same & (c2 <= r2), b_col - b_row + logi_row, -jnp.inf)
    inter = b_col + m_col
    m_t = jnp.maximum(inter, jnp.max(dmat, axis=1, keepdims=True))
    w_inter = jnp.exp(inter - m_t)

    q = q_ref[...]
    k = k_ref[...]
    v = v_ref[...]
    q32 = q.astype(F32)
    qk = lax.dot_general(q, k, (((1,), (1,)), ((), ())), preferred_element_type=F32) * scale
    s = qk * jnp.exp(dmat - m_t)
    num_s = jnp.dot(s.astype(BF16), v, preferred_element_type=F32)

    seq_of_row = lax.broadcasted_iota(jnp.int32, (R, 1), 0) // seq_len
    num_c = jnp.zeros(num_s.shape, F32)
    qn = jnp.zeros((R, 1), F32)
    for j in range(nseq):
        mine = seq_of_row == j
        cq = lax.dot_general(q, c_ref[j, 0].astype(BF16), (((1,), (1,)), ((), ())),
                             preferred_element_type=F32)
        num_c = jnp.where(mine, cq, num_c)
        qn = jnp.where(mine, jnp.sum(q32 * n_ref[j, 0], axis=1, keepdims=True), qn)
    num = w_inter * (num_c * scale) + num_s
    den = w_inter * (qn * scale) + jnp.sum(s, axis=1, keepdims=True)
    hh = num / jnp.maximum(jnp.abs(den), jnp.exp(-m_t))
    hm_ref[...] = _head_norm_gate(hh, o_ref[...], mhg_ref[...])

    pick_last = same & (c2 % seq_len == seq_len - 1)

    def last_of_seq(col):
        return jnp.sum(jnp.where(pick_last, _col_to_row(col, eye), 0.0), axis=1, keepdims=True)

    m_new = last_of_seq(m_t)
    w_c = jnp.exp(last_of_seq(inter) - m_new)
    w_s = jnp.exp(last_of_seq(b_col) - b_col + logi_col - m_new)
    vs = (w_s * v.astype(F32)).astype(BF16)
    kw = w_s * k.astype(F32)
    for j in range(nseq):
        mine = seq_of_row == j
        upd = lax.dot_general(jnp.where(mine, vs, jnp.zeros_like(vs)), k,
                              (((0,), (0,)), ((), ())), preferred_element_type=F32)
        wc_j = w_c[j * seq_len:j * seq_len + 1, :]
        c_out[j, 0] = wc_j * c_ref[j, 0] + upd
        n_out[j, 0] = wc_j * n_ref[j, 0] + jnp.sum(jnp.where(mine, kw, 0.0), axis=0, keepdims=True)
    m_out[0] = m_new


def _mlstm_sample(qkvo, gates, mh_g, state_c, state_n, m_rep, hm, c_prev, layer, row0,
                  n_seq, seq_len, n_heads, dqk, dv):
    nb = _tile(n_seq, SAMPLE_SEQS, 1)
    R = nb * seq_len
    assert row0 % R == 0
    blk0 = row0 // R
    qw, vw = n_heads * dqk, n_heads * dv
    o_blk0 = (2 * qw + vw) // dv
    ts = n_seq * seq_len
    any_spec = pl.BlockSpec(memory_space=pl.ANY)

    in_specs = [
        pl.BlockSpec((R, dqk), lambda i, h: (blk0 + i, h)),
        pl.BlockSpec((R, dqk), lambda i, h: (blk0 + i, n_heads + h)),
        pl.BlockSpec((R, dv), lambda i, h: (blk0 + i, 2 * qw // dv + h)),
        pl.BlockSpec((R, dv), lambda i, h: (blk0 + i, o_blk0 + h)),
        pl.BlockSpec((R, LANES), lambda i, h: (blk0 + i, 0)),
        pl.BlockSpec((None, 1, dv), lambda i, h: (layer, 0, h)),
        pl.BlockSpec((None, nb, 1, dv, dqk), lambda i, h: (layer, i, h, 0, 0)),
        pl.BlockSpec((None, nb, 1, 1, dqk), lambda i, h: (layer, i, h, 0, 0)),
        pl.BlockSpec((R, n_heads), lambda i, h: (i, 0)),
        any_spec,
    ]
    args = [qkvo, qkvo, qkvo, qkvo, gates, mh_g, state_c, state_n, m_rep, hm]
    aliases = {len(args) - 1: 0}
    if c_prev is not None:
        in_specs.append(any_spec)
        args.append(c_prev)
        aliases[len(args) - 1] = 1

    return pl.pallas_call(
        functools.partial(_mlstm_sample_kernel, n_heads=n_heads, scale=dqk ** -0.5,
                          seq_len=seq_len),
        grid=(n_seq // nb, n_heads),
        in_specs=in_specs,
        out_specs=[
            pl.BlockSpec((R, dv), lambda i, h: (blk0 + i, h)),
            pl.BlockSpec((None, nb, 1, dv, dqk), lambda i, h: (layer, i, h, 0, 0)),
            pl.BlockSpec((nb, 1, 1, dqk), lambda i, h: (i, h, 0, 0)),
            pl.BlockSpec((1, R, 1), lambda i, h: (h, i, 0)),
        ],
        out_shape=[
            jax.ShapeDtypeStruct(hm.shape, hm.dtype),
            jax.ShapeDtypeStruct(state_c.shape, F32),
            jax.ShapeDtypeStruct(state_n.shape[1:], F32),
            jax.ShapeDtypeStruct((n_heads, ts, 1), F32),
        ],
        input_output_aliases=aliases,
        compiler_params=_params(2),
        name="mlstm_sample",
    )(*args)


def _pool_conv_prompt_kernel(u_ref, uh_ref, cb_ref, cc_ref, cx_ref, cch_ref, cxh_ref,
                             maps_ref, scale_ref, cw_ref,
                             zp_ref, zs_ref, ps_ref, cs_ref, full_s, pf_s, *, windows):
    t = pl.program_id(1)
    last = pl.num_programs(1) - 1
    tr = u_ref.shape[0]
    gw = maps_ref.shape[1]
    n_buf = ps_ref.shape[1]
    taps = cw_ref.shape[0]
    has_past = t > 0

    u = u_ref[...]
    full_s[0:POOL_HALO, :] = jnp.where(has_past, uh_ref[...], 0.0)
    full_s[POOL_HALO:, :] = u
    pos = t * tr + lax.broadcasted_iota(jnp.int32, (tr, 1), 0)
    for g, w in enumerate(windows):
        cols = slice(g * gw, (g + 1) * gw)
        acc = u[:, cols]
        for j in range(1, w):
            acc = acc + full_s[POOL_HALO - j:POOL_HALO - j + tr, cols]
        cnt = jnp.minimum(pos + 1, w).astype(F32)
        zf = acc / cnt - u[:, cols]
        zp = jnp.dot(zf.astype(BF16), maps_ref[g], preferred_element_type=F32) * scale_ref[:, cols]
        zp_ref[:, cols] = zp.astype(BF16)

    p = cc_ref[...] * cx_ref[...]
    pf_s[0:CONV_HALO, :] = jnp.where(has_past, cch_ref[...] * cxh_ref[...], 0.0)
    pf_s[CONV_HALO:, :] = p
    y = p * cw_ref[taps - 1:taps, :]
    for j in range(taps - 1):
        back = taps - 1 - j
        y = y + pf_s[CONV_HALO - back:CONV_HALO - back + tr, :] * cw_ref[j:j + 1, :]
    zs_ref[...] = (cb_ref[...] * y).astype(BF16)

    @pl.when(t == last)
    def _():
        ps_ref[0] = full_s[POOL_HALO + tr - n_buf:POOL_HALO + tr, :]
        cs_ref[0] = pf_s[CONV_HALO + tr - (taps - 1):CONV_HALO + tr, :]


def _pool_conv_prompt(tail, maps, pscale, conv_w, layer, batch, seq, width, n_buf):
    tr = _tile(seq, 256, POOL_HALO)
    nt = seq // tr
    taps = conv_w.shape[1]
    n_groups, gw = maps.shape[1], maps.shape[2]
    hp = tr // POOL_HALO
    hc = tr // CONV_HALO

    def rows(b, t):
        return b * nt + t

    def main(col):
        return pl.BlockSpec((tr, width), lambda b, t: (rows(b, t), col))

    def halo(col, h, per_tile):
        return pl.BlockSpec((h, width), lambda b, t: (jnp.maximum(rows(b, t) * per_tile - 1, 0), col))

    return pl.pallas_call(
        functools.partial(_pool_conv_prompt_kernel, windows=POOL_WINDOWS),
        grid=(batch, nt),
        in_specs=[main(0), halo(0, POOL_HALO, hp), main(1), main(2), main(3),
                  halo(2, CONV_HALO, hc), halo(3, CONV_HALO, hc),
                  pl.BlockSpec((None, n_groups, gw, gw), lambda b, t: (layer, 0, 0, 0)),
                  pl.BlockSpec((None, 1, width), lambda b, t: (layer, 0, 0)),
                  pl.BlockSpec((None, taps, width), lambda b, t: (layer, 0, 0))],
        out_specs=[pl.BlockSpec((tr, width), lambda b, t: (rows(b, t), 0)),
                   pl.BlockSpec((tr, width), lambda b, t: (rows(b, t), 0)),
                   pl.BlockSpec((1, n_buf, width), lambda b, t: (b, 0, 0)),
                   pl.BlockSpec((1, taps - 1, width), lambda b, t: (b, 0, 0))],
        out_shape=[jax.ShapeDtypeStruct((tail.shape[0], width), BF16),
                   jax.ShapeDtypeStruct((tail.shape[0], width), BF16),
                   jax.ShapeDtypeStruct((batch, n_buf, width), F32),
                   jax.ShapeDtypeStruct((batch, taps - 1, width), F32)],
        scratch_shapes=[pltpu.VMEM((POOL_HALO + tr, width), F32),
                        pltpu.VMEM((CONV_HALO + tr, width), F32)],
        compiler_params=_params(2),
        name="pool_conv_prompt",
    )(tail, tail, tail, tail, tail, tail, tail, maps, pscale, conv_w)


def _pool_conv_sample_kernel(x_ref, ps_ref, cs_ref, maps_ref, scale_ref, cw_ref,
                             zp_ref, zs_ref, pn_ref, cn_ref, *, windows, start_pos):
    g = pl.program_id(0)
    seq_len = x_ref.shape[1]
    n_buf = ps_ref.shape[0]
    taps = cw_ref.shape[0]
    win = jnp.int32(windows[0])
    for i, w in enumerate(windows):
        win = jnp.where(g == i, jnp.int32(w), win)

    hist = [ps_ref[j] for j in range(n_buf)] + [x_ref[0, t] for t in range(seq_len)]
    for t in range(seq_len):
        cur = n_buf + t
        acc = hist[cur]
        for j in range(1, max(windows)):
            acc = acc + jnp.where(j < win, hist[cur - j], 0.0)
        cnt = jnp.minimum(start_pos + t + 1, win).astype(F32)
        zf = acc / cnt - hist[cur]
        zp = jnp.dot(zf.astype(BF16), maps_ref[0], preferred_element_type=F32) * scale_ref[...]
        zp_ref[t] = zp.astype(BF16)
    for j in range(n_buf):
        pn_ref[j] = hist[seq_len + j]

    prod = [cs_ref[j] for j in range(taps - 1)] + [x_ref[2, t] * x_ref[3, t] for t in range(seq_len)]
    for t in range(seq_len):
        y = prod[t] * cw_ref[0:1, :]
        for j in range(1, taps):
            y = y + prod[t + j] * cw_ref[j:j + 1, :]
        zs_ref[t] = (x_ref[1, t] * y).astype(BF16)
    for j in range(taps - 1):
        cn_ref[j] = prod[seq_len + j]


def _pool_conv_sample(x4, pstate_t, cstate_t, maps, pscale, conv_w, layer):
    _, seq_len, n_seq, width = x4.shape
    n_buf = pstate_t.shape[0]
    taps = conv_w.shape[1]
    n_groups, gw = maps.shape[1], maps.shape[2]
    assert len(POOL_WINDOWS) == n_groups and n_buf >= max(POOL_WINDOWS) - 1

    return pl.pallas_call(
        functools.partial(_pool_conv_sample_kernel, windows=POOL_WINDOWS, start_pos=PAST_LEN),
        grid=(n_groups,),
        in_specs=[pl.BlockSpec((4, seq_len, n_seq, gw), lambda g: (0, 0, 0, g)),
                  pl.BlockSpec((n_buf, n_seq, gw), lambda g: (0, 0, g)),
                  pl.BlockSpec((taps - 1, n_seq, gw), lambda g: (0, 0, g)),
                  pl.BlockSpec((None, 1, gw, gw), lambda g: (layer, g, 0, 0)),
                  pl.BlockSpec((None, 1, gw), lambda g: (layer, 0, g)),
                  pl.BlockSpec((None, taps, gw), lambda g: (layer, 0, g))],
        out_specs=[pl.BlockSpec((seq_len, n_seq, gw), lambda g: (0, 0, g)),
                   pl.BlockSpec((seq_len, n_seq, gw), lambda g: (0, 0, g)),
                   pl.BlockSpec((n_buf, n_seq, gw), lambda g: (0, 0, g)),
                   pl.BlockSpec((taps - 1, n_seq, gw), lambda g: (0, 0, g))],
        out_shape=[jax.ShapeDtypeStruct((seq_len, n_seq, width), BF16),
                   jax.ShapeDtypeStruct((seq_len, n_seq, width), BF16),
                   jax.ShapeDtypeStruct(pstate_t.shape, F32),
                   jax.ShapeDtypeStruct(cstate_t.shape, F32)],
        compiler_params=_params(1),
        name="pool_conv_sample",
    )(x4, pstate_t, cstate_t, maps, pscale, conv_w)


def kernel(x_prompt, x_sample, state_C, state_n, state_m, state_pool, state_conv, w_in, b_in, mh_g, pool_maps, pool_scale, conv_w, w_bm, w_bp, w_bs, w_o, ln1_g, ln1_b, w_gate, w_up, w_down, ln2_g, ln2_b):
    depth = w_in.shape[0]
    batch, seq, d = x_prompt.shape
    n_seq, seq_len, _ = x_sample.shape
    n_heads, dv, dqk = state_C.shape[2], state_C.shape[3], state_C.shape[4]
    n_buf, pw = state_pool.shape[2], state_pool.shape[3]
    cw = conv_w.shape[2]
    dff = w_gate.shape[2]
    assert pw == cw and dv == 2 * dqk and 2 * n_heads <= LANES
    alpha = (2 * depth) ** 0.25
    tp, ts = batch * seq, n_seq * seq_len
    t_all = tp + ts

    off_i = 2 * n_heads * dqk + 2 * n_heads * dv
    off_p = off_i + 2 * n_heads
    gate_col0 = pw + 3 * cw
    dff_pad = -(-dff // FF_ALIGN) * FF_ALIGN if dff > FF_ALIGN else dff
    k_blocks = 2 if dff_pad % (2 * LANES) == 0 else 1

    w_in_t = jnp.swapaxes(w_in, 1, 2)
    w_qkvo = _to_bf16_transposed(w_in_t, 0, off_i)
    w_if = jnp.pad(w_in[:, :, off_i:off_p], ((0, 0), (0, 0), (0, LANES - (off_p - off_i)))).astype(BF16)
    w_tail = _to_bf16_transposed(w_in_t, off_p, w_in.shape[2] - off_p)
    b3 = b_in[:, None, :]
    b_qkvo, b_tail = b3[:, :, :off_i], b3[:, :, off_p:]
    b_if = jnp.pad(b3[:, :, off_i:off_p], ((0, 0), (0, 0), (0, LANES - (off_p - off_i))))
    wbm, wbp, wbs, wo = (_to_bf16(w) for w in (w_bm, w_bp, w_bs, w_o))
    wg = _to_bf16(w_gate, cols_out=dff_pad)
    wu = _to_bf16(w_up, cols_out=dff_pad)
    wd = _to_bf16(w_down, rows_out=dff_pad)
    maps = pool_maps.astype(BF16)
    mhg3, pscale3 = mh_g[:, None, :], pool_scale[:, None, :]
    ln1g, ln1b, ln2g, ln2b = (a[:, None, :] for a in (ln1_g, ln1_b, ln2_g, ln2_b))
    state_n5 = state_n[:, :, :, None, :]
    pool_t = state_pool.transpose(0, 2, 1, 3)
    conv_t = state_conv.transpose(0, 2, 1, 3)

    x = jnp.concatenate([x_prompt.reshape(tp, d), x_sample.reshape(ts, d)], axis=0)
    xb = x.astype(BF16)

    outs = [[] for _ in range(9)]
    c_s = None
    for l in range(depth):
        qkvo = _matmul(xb, w_qkvo, l, bias=b_qkvo, out_dtype=BF16, tn=1024, name="in_qkvo")
        gates = _matmul(xb, w_if, l, bias=b_if, name="in_if")
        tail = _matmul(xb, w_tail, l, bias=b_tail, tn=1024, name="in_tail")

        hm, c_p, n_p, m_p = _mlstm_prompt(qkvo, gates, mhg3, l, batch, seq, t_all, n_heads, dqk, dv)
        m_rep = jnp.repeat(state_m[l], seq_len, axis=0)
        hm, c_s, n_s, m_s = _mlstm_sample(qkvo, gates, mhg3, state_C, state_n5, m_rep, hm, c_s, l,
                                          tp, n_seq, seq_len, n_heads, dqk, dv)

        zp, zs, ps_p, cs_p = _pool_conv_prompt(tail, maps, pscale3, conv_w, l, batch, seq, pw, n_buf)
        x4 = tail[tp:, :pw + 3 * cw].reshape(n_seq, seq_len, 4, pw).transpose(2, 1, 0, 3)
        zp_s, zs_s, ps_s, cs_s = _pool_conv_sample(x4, pool_t[l], conv_t[l], maps, pscale3, conv_w, l)
        zp = lax.dynamic_update_slice(zp, zp_s.transpose(1, 0, 2).reshape(ts, pw), (tp, 0))
        zs = lax.dynamic_update_slice(zs, zs_s.transpose(1, 0, 2).reshape(ts, cw), (tp, 0))

        mixed = _merge(hm, zp, zs, wbm, wbp, wbs, l, tail, gate_col0)
        r1 = _matmul(mixed, wo, l, res=x, res_scale=alpha, name="out_proj")
        x1, x1b = _layernorm(r1, ln1g, ln1b, l)

        ff = _ffn_up(x1b, wg, wu, l)
        r2 = _matmul(ff, wd, l, res=x1, res_scale=alpha, k_blocks=k_blocks, name="ffn_down")
        for kb in range(1, k_blocks):
            r2 = _matmul(ff, wd, l, res=r2, k_blocks=k_blocks, k_index=kb, name="ffn_down")
        if l + 1 < depth:
            x, xb = _layernorm(r2, ln2g, ln2b, l)
        else:
            y_p, = _layernorm(r2, ln2g, ln2b, l, 0, tp, with_bf16=False)
            y_s, = _layernorm(r2, ln2g, ln2b, l, tp, ts, with_bf16=False)

        new = (c_p, n_p, m_p[:, :, 0], ps_p, cs_p,
               n_s[:, :, 0, :], m_s[:, seq_len - 1::seq_len, 0].T,
               ps_s.transpose(1, 0, 2), cs_s.transpose(1, 0, 2))
        for acc, val in zip(outs, new):
            acc.append(val)

    st = [jnp.stack(o) for o in outs]
    return (y_p.reshape(batch, seq, d), y_s.reshape(n_seq, seq_len, d),
            st[0], st[1], st[2], st[3], st[4], c_s, st[5], st[6], st[7], st[8])
```

```python
import functools

import jax
import jax.numpy as jnp
from jax import lax
from jax.experimental import pallas as pl
from jax.experimental.pallas import tpu as pltpu

F32 = jnp.float32
BF16 = jnp.bfloat16

POOL_WINDOWS = (2, 4, 8, 16)
PAST_LEN = 16384
LN_EPS = 1e-5
PROMPT_CHUNK = 128

LANES = 128
BF16_SUBLANES = 16
VMEM_LIMIT = 56 * 1024 * 1024
POOL_HALO = 16
CONV_HALO = 8
SAMPLE_SEQS = 8
FF_TILE = 256


def _params(n_axes):
    return pltpu.CompilerParams(dimension_semantics=("arbitrary",) * n_axes,
                                vmem_limit_bytes=VMEM_LIMIT)


def _tile(dim, target, mult):
    best = None
    for d in range(mult, min(dim, target) + 1, mult):
        if dim % d == 0:
            best = d
    return dim if best is None else best


def _log_sigmoid(x):
    return jnp.minimum(x, 0.0) - jnp.log(1.0 + jnp.exp(-jnp.abs(x)))


def _cast_kernel(x_ref, o_ref):
    o_ref[...] = x_ref[...].astype(o_ref.dtype)


def _to_bf16(w):
    layers, rows, cols = w.shape
    tc = _tile(cols, 4096, LANES)
    tr = _tile(rows, max(BF16_SUBLANES, 1024 * 1024 // tc), BF16_SUBLANES)
    spec = pl.BlockSpec((None, tr, tc), lambda l, i, j: (l, i, j))
    return pl.pallas_call(
        _cast_kernel,
        grid=(layers, rows // tr, cols // tc),
        in_specs=[spec],
        out_specs=spec,
        out_shape=jax.ShapeDtypeStruct((layers, rows, cols), BF16),
        compiler_params=_params(3),
        name="to_bf16",
    )(w)


def _cast_transposed_kernel(a_ref, *rest, shift):
    o_ref = rest[-1]
    a = a_ref[...]
    if shift:
        a = jnp.concatenate([a[shift:], rest[0][...]], axis=0)
    o_ref[...] = a.T.astype(o_ref.dtype)


def _to_bf16_transposed(wt, row0, nrows, tn=512):
    layers, _, k = wt.shape
    shift = row0 % LANES
    base = row0 - shift
    tn = _tile(nrows, tn, LANES)
    assert base % tn == 0 and shift % 8 == 0
    in_specs = [pl.BlockSpec((None, tn, k), lambda l, j: (l, base // tn + j, 0))]
    args = [wt]
    if shift:
        in_specs.append(pl.BlockSpec((None, shift, k),
                                     lambda l, j: (l, (base + (j + 1) * tn) // shift, 0)))
        args.append(wt)
    return pl.pallas_call(
        functools.partial(_cast_transposed_kernel, shift=shift),
        grid=(layers, nrows // tn),
        in_specs=in_specs,
        out_specs=pl.BlockSpec((None, k, tn), lambda l, j: (l, 0, j)),
        out_shape=jax.ShapeDtypeStruct((layers, k, nrows), BF16),
        compiler_params=_params(2),
        name="to_bf16_transposed",
    )(*args)


def _mm_kernel(*refs, has_bias, has_res, res_scale):
    x_ref, w_ref = refs[0], refs[1]
    o_ref = refs[-1]
    acc = jnp.dot(x_ref[...], w_ref[...].astype(BF16), preferred_element_type=F32)
    pos = 2
    if has_bias:
        acc = acc + refs[pos][...]
        pos += 1
    if has_res:
        acc = acc + res_scale * refs[pos][...]
    o_ref[...] = acc.astype(o_ref.dtype)


def _matmul(x, w, layer, bias=None, res=None, res_scale=1.0, out_dtype=F32, tm=1088, tn=512,
            k_blocks=1, k_index=0, name="matmul"):
    t = x.shape[0]
    n = w.shape[2]
    k = x.shape[1] // k_blocks
    tm = _tile(t, tm, BF16_SUBLANES)
    tn = _tile(n, tn, LANES)
    in_specs = [pl.BlockSpec((tm, k), lambda i, j: (i, k_index)),
                pl.BlockSpec((None, k, tn), lambda i, j: (layer, k_index, j))]
    args = [x, w]
    if bias is not None:
        in_specs.append(pl.BlockSpec((None, 1, tn), lambda i, j: (layer, 0, j)))
        args.append(bias)
    if res is not None:
        in_specs.append(pl.BlockSpec((tm, tn), lambda i, j: (i, j)))
        args.append(res)
    return pl.pallas_call(
        functools.partial(_mm_kernel, has_bias=bias is not None, has_res=res is not None,
                          res_scale=res_scale),
        grid=(t // tm, n // tn),
        in_specs=in_specs,
        out_specs=pl.BlockSpec((tm, tn), lambda i, j: (i, j)),
        out_shape=jax.ShapeDtypeStruct((t, n), out_dtype),
        compiler_params=_params(2),
        name=name,
    )(*args)


def _ln_kernel(r_ref, g_ref, b_ref, of_ref, *maybe_bf16_ref):
    r = r_ref[...]
    mu = jnp.mean(r, axis=1, keepdims=True)
    d = r - mu
    var = jnp.mean(d * d, axis=1, keepdims=True)
    y = d * lax.rsqrt(var + LN_EPS) * g_ref[...] + b_ref[...]
    of_ref[...] = y
    for ob_ref in maybe_bf16_ref:
        ob_ref[...] = y.astype(BF16)


def _layernorm(r, g, b, layer, row0=0, n_rows=None, with_bf16=True):
    d = r.shape[1]
    n_rows = r.shape[0] - row0 if n_rows is None else n_rows
    tr = BF16_SUBLANES
    for cand in range(BF16_SUBLANES, 272 + 1, BF16_SUBLANES):
        if row0 % cand == 0 and n_rows % cand == 0:
            tr = cand
    blk0 = row0 // tr
    vec = pl.BlockSpec((None, 1, d), lambda i: (layer, 0, 0))
    out_row = pl.BlockSpec((tr, d), lambda i: (i, 0))
    out_specs = [out_row]
    out_shape = [jax.ShapeDtypeStruct((n_rows, d), F32)]
    if with_bf16:
        out_specs.append(out_row)
        out_shape.append(jax.ShapeDtypeStruct((n_rows, d), BF16))
    return pl.pallas_call(
        _ln_kernel,
        grid=(n_rows // tr,),
        in_specs=[pl.BlockSpec((tr, d), lambda i: (blk0 + i, 0)), vec, vec],
        out_specs=out_specs,
        out_shape=out_shape,
        compiler_params=_params(1),
        name="layernorm",
    )(r, g, b)


def _ffn_up_kernel(x_ref, wg_ref, wu_ref, o_ref):
    x = x_ref[...]
    a = jnp.dot(x, wg_ref[...].astype(BF16), preferred_element_type=F32)
    u = jnp.dot(x, wu_ref[...].astype(BF16), preferred_element_type=F32)
    o_ref[...] = (a * jax.nn.sigmoid(a) * u).astype(o_ref.dtype)


def _ffn_up(x, wg, wu, layer, tm=1088, tn=512):
    t, k = x.shape
    n = wg.shape[2]
    tm = _tile(t, tm, BF16_SUBLANES)
    tn = _tile(n, tn, LANES)
    wspec = pl.BlockSpec((None, k, tn), lambda i, j: (layer, 0, j))
    return pl.pallas_call(
        _ffn_up_kernel,
        grid=(t // tm, n // tn),
        in_specs=[pl.BlockSpec((tm, k), lambda i, j: (i, 0)), wspec, wspec],
        out_specs=pl.BlockSpec((tm, tn), lambda i, j: (i, j)),
        out_shape=jax.ShapeDtypeStruct((t, n), BF16),
        compiler_params=_params(2),
        name="ffn_up",
    )(x, wg, wu)


def _merge_kernel(hm_ref, zp_ref, zs_ref, wm_ref, wp_ref, ws_ref, gm_ref, gp_ref, gs_ref, o_ref):
    ym = jnp.dot(hm_ref[...], wm_ref[...], preferred_element_type=F32)
    yp = jnp.dot(zp_ref[...], wp_ref[...], preferred_element_type=F32)
    ys = jnp.dot(zs_ref[...], ws_ref[...], preferred_element_type=F32)
    mixed = (jax.nn.sigmoid(gm_ref[...]) * ym + jax.nn.sigmoid(gp_ref[...]) * yp
             + jax.nn.sigmoid(gs_ref[...]) * ys)
    o_ref[...] = mixed.astype(o_ref.dtype)


def _merge(hm, zp, zs, wm, wp, ws, layer, tail, gate_col0, tm=544, tn=512):
    t = hm.shape[0]
    d = wm.shape[2]
    tm = _tile(t, tm, BF16_SUBLANES)
    tn = _tile(d, tn, LANES)
    g0 = gate_col0 // tn
    gd = d // tn

    def xspec(a):
        return pl.BlockSpec((tm, a.shape[1]), lambda i, j: (i, 0))

    def wspec(a):
        return pl.BlockSpec((None, a.shape[1], tn), lambda i, j: (layer, 0, j))

    def gspec(branch):
        return pl.BlockSpec((tm, tn), lambda i, j: (i, g0 + branch * gd + j))

    return pl.pallas_call(
        _merge_kernel,
        grid=(t // tm, d // tn),
        in_specs=[xspec(hm), xspec(zp), xspec(zs), wspec(wm), wspec(wp), wspec(ws),
                  gspec(0), gspec(1), gspec(2)],
        out_specs=pl.BlockSpec((tm, tn), lambda i, j: (i, j)),
        out_shape=jax.ShapeDtypeStruct((t, d), BF16),
        compiler_params=_params(2),
        name="merge",
    )(hm, zp, zs, wm, wp, ws, tail, tail, tail)


def _pick_lane(x, lane_index):
    lane = lax.broadcasted_iota(jnp.int32, x.shape, 1)
    return jnp.sum(jnp.where(lane == lane_index, x, 0.0), axis=1, keepdims=True)


def _col_to_row(col, eye):
    return jnp.sum(jnp.where(eye, col, 0.0), axis=0, keepdims=True)


def _segment_cumsum(x, seg_pos, seg_len):
    s = 1
    while s < seg_len:
        x = x + jnp.where(seg_pos >= s, pltpu.roll(x, s, axis=0), 0.0)
        s *= 2
    return x


def _head_norm_gate(hh, o_pre, mh_g):
    mu = jnp.mean(hh, axis=1, keepdims=True)
    d = hh - mu
    var = jnp.mean(d * d, axis=1, keepdims=True)
    return (jax.nn.sigmoid(o_pre.astype(F32)) * (d * lax.rsqrt(var + LN_EPS) * mh_g)).astype(BF16)


def _mlstm_prompt_kernel(q_ref, k_ref, v_ref, o_ref, g_ref, mhg_ref,
                         hm_ref, c_out, n_out, m_out,
                         ct_s, n_s, m_s, *, n_heads, dqk, dv, scale):
    c = pl.program_id(1)
    last = pl.num_programs(1) - 1
    L = q_ref.shape[0]

    @pl.when(c == 0)
    def _():
        ct_s[...] = jnp.zeros_like(ct_s)
        n_s[...] = jnp.zeros_like(n_s)
        m_s[...] = jnp.zeros_like(m_s)

    g = g_ref[...]
    row = lax.broadcasted_iota(jnp.int32, g.shape, 0)
    lane = lax.broadcasted_iota(jnp.int32, g.shape, 1)
    bsum = _segment_cumsum(_log_sigmoid(g), row, L)
    gb = jnp.where(lane < n_heads, g, bsum)
    gb_t = gb.T
    r2 = lax.broadcasted_iota(jnp.int32, (L, L), 0)
    c2 = lax.broadcasted_iota(jnp.int32, (L, L), 1)
    causal = r2 >= c2

    for h in range(n_heads):
        logi_col = gb[:, h:h + 1]
        b_col = gb[:, n_heads + h:n_heads + h + 1]
        logi_row = gb_t[h:h + 1, :]
        b_row = gb_t[n_heads + h:n_heads + h + 1, :]
        m_prev = m_s[h:h + 1, :]
        dmat = jnp.where(causal, b_col - b_row + logi_row, -jnp.inf)
        inter = b_col + m_prev
        m_t = jnp.maximum(inter, jnp.max(dmat, axis=1, keepdims=True))
        w_inter = jnp.exp(inter - m_t)

        q = q_ref[:, h * dqk:(h + 1) * dqk]
        k = k_ref[:, h * dqk:(h + 1) * dqk]
        v = v_ref[:, h * dv:(h + 1) * dv]
        ct = ct_s[h]
        n_prev = n_s[h:h + 1, :]
        qk = lax.dot_general(q, k, (((1,), (1,)), ((), ())), preferred_element_type=F32) * scale
        s = qk * jnp.exp(dmat - m_t)
        num = (w_inter * (jnp.dot(q, ct.astype(BF16), preferred_element_type=F32) * scale)
               + jnp.dot(s.astype(BF16), v, preferred_element_type=F32))
        qn = jnp.sum(q.astype(F32) * n_prev, axis=1, keepdims=True) * scale
        den = w_inter * qn + jnp.sum(s, axis=1, keepdims=True)
        hh = num / jnp.maximum(jnp.abs(den), jnp.exp(-m_t))
        hm_ref[:, h * dv:(h + 1) * dv] = _head_norm_gate(
            hh, o_ref[:, h * dv:(h + 1) * dv], mhg_ref[:, h * dv:(h + 1) * dv])

        m_new = m_t[L - 1:L, :]
        w_c = jnp.exp(inter[L - 1:L, :] - m_new)
        w_s = jnp.exp(b_col[L - 1:L, :] - b_col + logi_col - m_new)
        vs = (w_s * v.astype(F32)).astype(BF16)
        ct_new = w_c * ct + lax.dot_general(k, vs, (((0,), (0,)), ((), ())),
                                            preferred_element_type=F32)
        n_new = w_c * n_prev + jnp.sum(w_s * k.astype(F32), axis=0, keepdims=True)
        ct_s[h] = ct_new
        n_s[h:h + 1, :] = n_new
        m_s[h:h + 1, :] = m_new

    @pl.when(c == last)
    def _():
        for h in range(n_heads):
            c_out[0, h] = ct_s[h].T
        n_out[0] = n_s[...]
        m_out[0] = m_s[...]


def _mlstm_prompt(qkvo, gates, mh_g, layer, batch, seq, n_rows, n_heads, dqk, dv):
    L = PROMPT_CHUNK if seq % PROMPT_CHUNK == 0 else seq
    nc = seq // L
    qw, vw = n_heads * dqk, n_heads * dv
    assert qw % LANES == 0 and 2 * qw % vw == 0

    def rows(b, c):
        return b * nc + c

    return pl.pallas_call(
        functools.partial(_mlstm_prompt_kernel, n_heads=n_heads, dqk=dqk, dv=dv,
                          scale=dqk ** -0.5),
        grid=(batch, nc),
        in_specs=[
            pl.BlockSpec((L, qw), lambda b, c: (rows(b, c), 0)),
            pl.BlockSpec((L, qw), lambda b, c: (rows(b, c), 1)),
            pl.BlockSpec((L, vw), lambda b, c: (rows(b, c), 2 * qw // vw)),
            pl.BlockSpec((L, vw), lambda b, c: (rows(b, c), 2 * qw // vw + 1)),
            pl.BlockSpec((L, LANES), lambda b, c: (rows(b, c), 0)),
            pl.BlockSpec((None, 1, vw), lambda b, c: (layer, 0, 0)),
        ],
        out_specs=[
            pl.BlockSpec((L, vw), lambda b, c: (rows(b, c), 0)),
            pl.BlockSpec((1, n_heads, dv, dqk), lambda b, c: (b, 0, 0, 0)),
            pl.BlockSpec((1, n_heads, dqk), lambda b, c: (b, 0, 0)),
            pl.BlockSpec((1, n_heads, 1), lambda b, c: (b, 0, 0)),
        ],
        out_shape=[
            jax.ShapeDtypeStruct((n_rows, vw), BF16),
            jax.ShapeDtypeStruct((batch, n_heads, dv, dqk), F32),
            jax.ShapeDtypeStruct((batch, n_heads, dqk), F32),
            jax.ShapeDtypeStruct((batch, n_heads, 1), F32),
        ],
        scratch_shapes=[pltpu.VMEM((n_heads, dqk, dv), F32), pltpu.VMEM((n_heads, dqk), F32),
                        pltpu.VMEM((n_heads, 1), F32)],
        compiler_params=_params(2),
        name="mlstm_prompt",
    )(qkvo, qkvo, qkvo, qkvo, gates, mh_g)


def _mlstm_sample_kernel(q_ref, k_ref, v_ref, o_ref, g_ref, mhg_ref, c_ref, n_ref, mrep_ref,
                         *rest, n_heads, scale, seq_len):
    hm_ref, c_out, n_out, m_out = rest[-4:]
    h = pl.program_id(1)
    R = q_ref.shape[0]
    nseq = R // seq_len

    g = g_ref[...]
    row = lax.broadcasted_iota(jnp.int32, g.shape, 0)
    bsum = _segment_cumsum(_log_sigmoid(g), row % seq_len, seq_len)
    logi_col = _pick_lane(g, h)
    b_col = _pick_lane(bsum, h + n_heads)
    m_col = _pick_lane(mrep_ref[...], h)

    r2 = lax.broadcasted_iota(jnp.int32, (R, R), 0)
    c2 = lax.broadcasted_iota(jnp.int32, (R, R), 1)
    eye = r2 == c2
    same = (r2 // seq_len) == (c2 // seq_len)
    b_row = _col_to_row(b_col, eye)
    logi_row = _col_to_row(logi_col, eye)
    dmat = jnp.where(same & (c2 <= r2), b_col - b_row + logi_row, -jnp.inf)
    inter = b_col + m_col
    m_t = jnp.maximum(inter, jnp.max(dmat, axis=1, keepdims=True))
    w_inter = jnp.exp(inter - m_t)

    q = q_ref[...]
    k = k_ref[...]
    v = v_ref[...]
    q32 = q.astype(F32)
    qk = lax.dot_general(q, k, (((1,), (1,)), ((), ())), preferred_element_type=F32) * scale
    s = qk * jnp.exp(dmat - m_t)
    num_s = jnp.dot(s.astype(BF16), v, preferred_element_type=F32)

    seq_of_row = lax.broadcasted_iota(jnp.int32, (R, 1), 0) // seq_len
    num_c = jnp.zeros(num_s.shape, F32)
    qn = jnp.zeros((R, 1), F32)
    for j in range(nseq):
        mine = seq_of_row == j
        cq = lax.dot_general(q, c_ref[j, 0].astype(BF16), (((1,), (1,)), ((), ())),
                             preferred_element_type=F32)
        num_c = jnp.where(mine, cq, num_c)
        qn = jnp.where(mine, jnp.sum(q32 * n_ref[j, 0], axis=1, keepdims=True), qn)
    num = w_inter * (num_c * scale) + num_s
    den = w_inter * (qn * scale) + jnp.sum(s, axis=1, keepdims=True)
    hh = num / jnp.maximum(jnp.abs(den), jnp.exp(-m_t))
    hm_ref[...] = _head_norm_gate(hh, o_ref[...], mhg_ref[...])

    pick_last = same & (c2 % seq_len == seq_len - 1)

    def last_of_seq(col):
        return jnp.sum(jnp.where(pick_last, _col_to_row(col, eye), 0.0), axis=1, keepdims=True)

    m_new = last_of_seq(m_t)
    w_c = jnp.exp(last_of_seq(inter) - m_new)
    w_s = jnp.exp(last_of_seq(b_col) - b_col + logi_col - m_new)
    vs = (w_s * v.astype(F32)).astype(BF16)
    kw = w_s * k.astype(F32)
    for j in range(nseq):
        mine = seq_of_row == j
        upd = lax.dot_general(jnp.where(mine, vs, jnp.zeros_like(vs)), k,
                              (((0,), (0,)), ((), ())), preferred_element_type=F32)
        wc_j = w_c[j * seq_len:j * seq_len + 1, :]
        c_out[j, 0] = wc_j * c_ref[j, 0] + upd
        n_out[j, 0] = wc_j * n_ref[j, 0] + jnp.sum(jnp.where(mine, kw, 0.0), axis=0, keepdims=True)
    m_out[0] = m_new


def _mlstm_sample(qkvo, gates, mh_g, state_c, state_n, m_rep, hm, c_prev, layer, row0,
                  n_seq, seq_len, n_heads, dqk, dv):
    nb = _tile(n_seq, SAMPLE_SEQS, 1)
    R = nb * seq_len
    assert row0 % R == 0
    blk0 = row0 // R
    qw, vw = n_heads * dqk, n_heads * dv
    o_blk0 = (2 * qw + vw) // dv
    ts = n_seq * seq_len
    any_spec = pl.BlockSpec(memory_space=pl.ANY)

    in_specs = [
        pl.BlockSpec((R, dqk), lambda i, h: (blk0 + i, h)),
        pl.BlockSpec((R, dqk), lambda i, h: (blk0 + i, n_heads + h)),
        pl.BlockSpec((R, dv), lambda i, h: (blk0 + i, 2 * qw // dv + h)),
        pl.BlockSpec((R, dv), lambda i, h: (blk0 + i, o_blk0 + h)),
        pl.BlockSpec((R, LANES), lambda i, h: (blk0 + i, 0)),
        pl.BlockSpec((None, 1, dv), lambda i, h: (layer, 0, h)),
        pl.BlockSpec((None, nb, 1, dv, dqk), lambda i, h: (layer, i, h, 0, 0)),
        pl.BlockSpec((None, nb, 1, 1, dqk), lambda i, h: (layer, i, h, 0, 0)),
        pl.BlockSpec((R, n_heads), lambda i, h: (i, 0)),
        any_spec,
    ]
    args = [qkvo, qkvo, qkvo, qkvo, gates, mh_g, state_c, state_n, m_rep, hm]
    aliases = {len(args) - 1: 0}
    if c_prev is not None:
        in_specs.append(any_spec)
        args.append(c_prev)
        aliases[len(args) - 1] = 1

    return pl.pallas_call(
        functools.partial(_mlstm_sample_kernel, n_heads=n_heads, scale=dqk ** -0.5,
                          seq_len=seq_len),
        grid=(n_seq // nb, n_heads),
        in_specs=in_specs,
        out_specs=[
            pl.BlockSpec((R, dv), lambda i, h: (blk0 + i, h)),
            pl.BlockSpec((None, nb, 1, dv, dqk), lambda i, h: (layer, i, h, 0, 0)),
            pl.BlockSpec((nb, 1, 1, dqk), lambda i, h: (i, h, 0, 0)),
            pl.BlockSpec((1, R, 1), lambda i, h: (h, i, 0)),
        ],
        out_shape=[
            jax.ShapeDtypeStruct(hm.shape, hm.dtype),
            jax.ShapeDtypeStruct(state_c.shape, F32),
            jax.ShapeDtypeStruct(state_n.shape[1:], F32),
            jax.ShapeDtypeStruct((n_heads, ts, 1), F32),
        ],
        input_output_aliases=aliases,
        compiler_params=_params(2),
        name="mlstm_sample",
    )(*args)


def _pool_conv_prompt_kernel(u_ref, uh_ref, cb_ref, cc_ref, cx_ref, cch_ref, cxh_ref,
                             maps_ref, scale_ref, cw_ref,
                             zp_ref, zs_ref, ps_ref, cs_ref, full_s, pf_s, *, windows):
    t = pl.program_id(1)
    last = pl.num_programs(1) - 1
    tr = u_ref.shape[0]
    gw = maps_ref.shape[1]
    n_buf = ps_ref.shape[1]
    taps = cw_ref.shape[0]
    has_past = t > 0

    u = u_ref[...]
    full_s[0:POOL_HALO, :] = jnp.where(has_past, uh_ref[...], 0.0)
    full_s[POOL_HALO:, :] = u
    pos = t * tr + lax.broadcasted_iota(jnp.int32, (tr, 1), 0)
    for g, w in enumerate(windows):
        cols = slice(g * gw, (g + 1) * gw)
        acc = u[:, cols]
        for j in range(1, w):
            acc = acc + full_s[POOL_HALO - j:POOL_HALO - j + tr, cols]
        cnt = jnp.minimum(pos + 1, w).astype(F32)
        zf = acc / cnt - u[:, cols]
        zp = jnp.dot(zf.astype(BF16), maps_ref[g], preferred_element_type=F32) * scale_ref[:, cols]
        zp_ref[:, cols] = zp.astype(BF16)

    p = cc_ref[...] * cx_ref[...]
    pf_s[0:CONV_HALO, :] = jnp.where(has_past, cch_ref[...] * cxh_ref[...], 0.0)
    pf_s[CONV_HALO:, :] = p
    y = p * cw_ref[taps - 1:taps, :]
    for j in range(taps - 1):
        back = taps - 1 - j
        y = y + pf_s[CONV_HALO - back:CONV_HALO - back + tr, :] * cw_ref[j:j + 1, :]
    zs_ref[...] = (cb_ref[...] * y).astype(BF16)

    @pl.when(t == last)
    def _():
        ps_ref[0] = full_s[POOL_HALO + tr - n_buf:POOL_HALO + tr, :]
        cs_ref[0] = pf_s[CONV_HALO + tr - (taps - 1):CONV_HALO + tr, :]


def _pool_conv_prompt(tail, maps, pscale, conv_w, layer, batch, seq, width, n_buf):
    tr = _tile(seq, 256, POOL_HALO)
    nt = seq // tr
    taps = conv_w.shape[1]
    n_groups, gw = maps.shape[1], maps.shape[2]
    hp = tr // POOL_HALO
    hc = tr // CONV_HALO

    def rows(b, t):
        return b * nt + t

    def main(col):
        return pl.BlockSpec((tr, width), lambda b, t: (rows(b, t), col))

    def halo(col, h, per_tile):
        return pl.BlockSpec((h, width), lambda b, t: (jnp.maximum(rows(b, t) * per_tile - 1, 0), col))

    return pl.pallas_call(
        functools.partial(_pool_conv_prompt_kernel, windows=POOL_WINDOWS),
        grid=(batch, nt),
        in_specs=[main(0), halo(0, POOL_HALO, hp), main(1), main(2), main(3),
                  halo(2, CONV_HALO, hc), halo(3, CONV_HALO, hc),
                  pl.BlockSpec((None, n_groups, gw, gw), lambda b, t: (layer, 0, 0, 0)),
                  pl.BlockSpec((None, 1, width), lambda b, t: (layer, 0, 0)),
                  pl.BlockSpec((None, taps, width), lambda b, t: (layer, 0, 0))],
        out_specs=[pl.BlockSpec((tr, width), lambda b, t: (rows(b, t), 0)),
                   pl.BlockSpec((tr, width), lambda b, t: (rows(b, t), 0)),
                   pl.BlockSpec((1, n_buf, width), lambda b, t: (b, 0, 0)),
                   pl.BlockSpec((1, taps - 1, width), lambda b, t: (b, 0, 0))],
        out_shape=[jax.ShapeDtypeStruct((tail.shape[0], width), BF16),
                   jax.ShapeDtypeStruct((tail.shape[0], width), BF16),
                   jax.ShapeDtypeStruct((batch, n_buf, width), F32),
                   jax.ShapeDtypeStruct((batch, taps - 1, width), F32)],
        scratch_shapes=[pltpu.VMEM((POOL_HALO + tr, width), F32),
                        pltpu.VMEM((CONV_HALO + tr, width), F32)],
        compiler_params=_params(2),
        name="pool_conv_prompt",
    )(tail, tail, tail, tail, tail, tail, tail, maps, pscale, conv_w)


def _pool_conv_sample_kernel(x_ref, ps_ref, cs_ref, maps_ref, scale_ref, cw_ref,
                             zp_ref, zs_ref, pn_ref, cn_ref, *, windows, start_pos):
    g = pl.program_id(0)
    seq_len = x_ref.shape[1]
    n_buf = ps_ref.shape[0]
    taps = cw_ref.shape[0]
    win = jnp.int32(windows[0])
    for i, w in enumerate(windows):
        win = jnp.where(g == i, jnp.int32(w), win)

    hist = [ps_ref[j] for j in range(n_buf)] + [x_ref[0, t] for t in range(seq_len)]
    for t in range(seq_len):
        cur = n_buf + t
        acc = hist[cur]
        for j in range(1, max(windows)):
            acc = acc + jnp.where(j < win, hist[cur - j], 0.0)
        cnt = jnp.minimum(start_pos + t + 1, win).astype(F32)
        zf = acc / cnt - hist[cur]
        zp = jnp.dot(zf.astype(BF16), maps_ref[0], preferred_element_type=F32) * scale_ref[...]
        zp_ref[t] = zp.astype(BF16)
    for j in range(n_buf):
        pn_ref[j] = hist[seq_len + j]

    prod = [cs_ref[j] for j in range(taps - 1)] + [x_ref[2, t] * x_ref[3, t] for t in range(seq_len)]
    for t in range(seq_len):
        y = prod[t] * cw_ref[0:1, :]
        for j in range(1, taps):
            y = y + prod[t + j] * cw_ref[j:j + 1, :]
        zs_ref[t] = (x_ref[1, t] * y).astype(BF16)
    for j in range(taps - 1):
        cn_ref[j] = prod[seq_len + j]


def _pool_conv_sample(x4, pstate_t, cstate_t, maps, pscale, conv_w, layer):
    _, seq_len, n_seq, width = x4.shape
    n_buf = pstate_t.shape[0]
    taps = conv_w.shape[1]
    n_groups, gw = maps.shape[1], maps.shape[2]
    assert len(POOL_WINDOWS) == n_groups and n_buf >= max(POOL_WINDOWS) - 1

    return pl.pallas_call(
        functools.partial(_pool_conv_sample_kernel, windows=POOL_WINDOWS, start_pos=PAST_LEN),
        grid=(n_groups,),
        in_specs=[pl.BlockSpec((4, seq_len, n_seq, gw), lambda g: (0, 0, 0, g)),
                  pl.BlockSpec((n_buf, n_seq, gw), lambda g: (0, 0, g)),
                  pl.BlockSpec((taps - 1, n_seq, gw), lambda g: (0, 0, g)),
                  pl.BlockSpec((None, 1, gw, gw), lambda g: (layer, g, 0, 0)),
                  pl.BlockSpec((None, 1, gw), lambda g: (layer, 0, g)),
                  pl.BlockSpec((None, taps, gw), lambda g: (layer, 0, g))],
        out_specs=[pl.BlockSpec((seq_len, n_seq, gw), lambda g: (0, 0, g)),
                   pl.BlockSpec((seq_len, n_seq, gw), lambda g: (0, 0, g)),
                   pl.BlockSpec((n_buf, n_seq, gw), lambda g: (0, 0, g)),
                   pl.BlockSpec((taps - 1, n_seq, gw), lambda g: (0, 0, g))],
        out_shape=[jax.ShapeDtypeStruct((seq_len, n_seq, width), BF16),
                   jax.ShapeDtypeStruct((seq_len, n_seq, width), BF16),
                   jax.ShapeDtypeStruct(pstate_t.shape, F32),
                   jax.ShapeDtypeStruct(cstate_t.shape, F32)],
        compiler_params=_params(1),
        name="pool_conv_sample",
    )(x4, pstate_t, cstate_t, maps, pscale, conv_w)


def kernel(x_prompt, x_sample, state_C, state_n, state_m, state_pool, state_conv, w_in, b_in, mh_g, pool_maps, pool_scale, conv_w, w_bm, w_bp, w_bs, w_o, ln1_g, ln1_b, w_gate, w_up, w_down, ln2_g, ln2_b):
    depth = w_in.shape[0]
    batch, seq, d = x_prompt.shape
    n_seq, seq_len, _ = x_sample.shape
    n_heads, dv, dqk = state_C.shape[2], state_C.shape[3], state_C.shape[4]
    n_buf, pw = state_pool.shape[2], state_pool.shape[3]
    cw = conv_w.shape[2]
    dff = w_gate.shape[2]
    assert pw == cw and dv == 2 * dqk and 2 * n_heads <= LANES
    alpha = (2 * depth) ** 0.25
    tp, ts = batch * seq, n_seq * seq_len
    t_all = tp + ts

    off_i = 2 * n_heads * dqk + 2 * n_heads * dv
    off_p = off_i + 2 * n_heads
    gate_col0 = pw + 3 * cw
    k_blocks = 2 if dff % (2 * LANES) == 0 else 1

    w_in_t = jnp.swapaxes(w_in, 1, 2)
    w_qkvo = _to_bf16_transposed(w_in_t, 0, off_i)
    w_if = jnp.pad(w_in[:, :, off_i:off_p], ((0, 0), (0, 0), (0, LANES - (off_p - off_i)))).astype(BF16)
    w_tail = _to_bf16_transposed(w_in_t, off_p, w_in.shape[2] - off_p)
    b3 = b_in[:, None, :]
    b_qkvo, b_tail = b3[:, :, :off_i], b3[:, :, off_p:]
    b_if = jnp.pad(b3[:, :, off_i:off_p], ((0, 0), (0, 0), (0, LANES - (off_p - off_i))))
    wbm, wbp, wbs = (_to_bf16(w) for w in (w_bm, w_bp, w_bs))
    maps = pool_maps.astype(BF16)
    mhg3, pscale3 = mh_g[:, None, :], pool_scale[:, None, :]
    ln1g, ln1b, ln2g, ln2b = (a[:, None, :] for a in (ln1_g, ln1_b, ln2_g, ln2_b))
    state_n5 = state_n[:, :, :, None, :]
    pool_t = state_pool.transpose(0, 2, 1, 3)
    conv_t = state_conv.transpose(0, 2, 1, 3)

    x = jnp.concatenate([x_prompt.reshape(tp, d), x_sample.reshape(ts, d)], axis=0)
    xb = x.astype(BF16)

    outs = [[] for _ in range(9)]
    c_s = None
    for l in range(depth):
        qkvo = _matmul(xb, w_qkvo, l, bias=b_qkvo, out_dtype=BF16, tn=1024, name="in_qkvo")
        gates = _matmul(xb, w_if, l, bias=b_if, name="in_if")
        tail = _matmul(xb, w_tail, l, bias=b_tail, tn=1024, name="in_tail")

        hm, c_p, n_p, m_p = _mlstm_prompt(qkvo, gates, mhg3, l, batch, seq, t_all, n_heads, dqk, dv)
        m_rep = jnp.repeat(state_m[l], seq_len, axis=0)
        hm, c_s, n_s, m_s = _mlstm_sample(qkvo, gates, mhg3, state_C, state_n5, m_rep, hm, c_s, l,
                                          tp, n_seq, seq_len, n_heads, dqk, dv)

        zp, zs, ps_p, cs_p = _pool_conv_prompt(tail, maps, pscale3, conv_w, l, batch, seq, pw, n_buf)
        x4 = tail[tp:, :pw + 3 * cw].reshape(n_seq, seq_len, 4, pw).transpose(2, 1, 0, 3)
        zp_s, zs_s, ps_s, cs_s = _pool_conv_sample(x4, pool_t[l], conv_t[l], maps, pscale3, conv_w, l)
        zp = lax.dynamic_update_slice(zp, zp_s.transpose(1, 0, 2).reshape(ts, pw), (tp, 0))
        zs = lax.dynamic_update_slice(zs, zs_s.transpose(1, 0, 2).reshape(ts, cw), (tp, 0))

        mixed = _merge(hm, zp, zs, wbm, wbp, wbs, l, tail, gate_col0)
        r1 = _matmul(mixed, w_o, l, res=x, res_scale=alpha, name="out_proj")
        x1, x1b = _layernorm(r1, ln1g, ln1b, l)

        ff = _ffn_up(x1b, w_gate, w_up, l, tn=FF_TILE)
        r2 = _matmul(ff, w_down, l, res=x1, res_scale=alpha, tn=FF_TILE, k_blocks=k_blocks,
                     name="ffn_down")
        for kb in range(1, k_blocks):
            r2 = _matmul(ff, w_down, l, res=r2, tn=FF_TILE, k_blocks=k_blocks, k_index=kb,
                         name="ffn_down")
        if l + 1 < depth:
            x, xb = _layernorm(r2, ln2g, ln2b, l)
        else:
            y_p, = _layernorm(r2, ln2g, ln2b, l, 0, tp, with_bf16=False)
            y_s, = _layernorm(r2, ln2g, ln2b, l, tp, ts, with_bf16=False)

        new = (c_p, n_p, m_p[:, :, 0], ps_p, cs_p,
               n_s[:, :, 0, :], m_s[:, seq_len - 1::seq_len, 0].T,
               ps_s.transpose(1, 0, 2), cs_s.transpose(1, 0, 2))
        for acc, val in zip(outs, new):
            acc.append(val)

    st = [jnp.stack(o) for o in outs]
    return (y_p.reshape(batch, seq, d), y_s.reshape(n_seq, seq_len, d),
            st[0], st[1], st[2], st[3], st[4], c_s, st[5], st[6], st[7], st[8])
```

```python
import functools

import jax
import jax.numpy as jnp
from jax import lax
from jax.experimental import pallas as pl
from jax.experimental.pallas import tpu as pltpu

F32 = jnp.float32
BF16 = jnp.bfloat16

POOL_WINDOWS = (2, 4, 8, 16)
PAST_LEN = 16384
LN_EPS = 1e-5
PROMPT_CHUNK = 128

LANES = 128
BF16_SUBLANES = 16
VMEM_LIMIT = 56 * 1024 * 1024
POOL_HALO = 16
CONV_HALO = 8
SAMPLE_SEQS = 8
FF_TILE = 256


def _params(n_axes):
    return pltpu.CompilerParams(dimension_semantics=("arbitrary",) * n_axes,
                                vmem_limit_bytes=VMEM_LIMIT)


def _tile(dim, target, mult):
    best = None
    for d in range(mult, min(dim, target) + 1, mult):
        if dim % d == 0:
            best = d
    return dim if best is None else best


def _log_sigmoid(x):
    return jnp.minimum(x, 0.0) - jnp.log(1.0 + jnp.exp(-jnp.abs(x)))


def _cast_kernel(x_ref, o_ref):
    o_ref[...] = x_ref[...].astype(o_ref.dtype)


def _to_bf16(w):
    layers, rows, cols = w.shape
    tc = _tile(cols, 4096, LANES)
    tr = _tile(rows, max(BF16_SUBLANES, 1024 * 1024 // tc), BF16_SUBLANES)
    spec = pl.BlockSpec((None, tr, tc), lambda l, i, j: (l, i, j))
    return pl.pallas_call(
        _cast_kernel,
        grid=(layers, rows // tr, cols // tc),
        in_specs=[spec],
        out_specs=spec,
        out_shape=jax.ShapeDtypeStruct((layers, rows, cols), BF16),
        compiler_params=_params(3),
        name="to_bf16",
    )(w)


def _mm_kernel(*refs, has_bias, has_res, res_scale):
    x_ref, w_ref = refs[0], refs[1]
    o_ref = refs[-1]
    acc = jnp.dot(x_ref[...], w_ref[...].astype(BF16), preferred_element_type=F32)
    pos = 2
    if has_bias:
        acc = acc + refs[pos][...]
        pos += 1
    if has_res:
        acc = acc + res_scale * refs[pos][...]
    o_ref[...] = acc.astype(o_ref.dtype)


def _matmul(x, w, layer, bias=None, res=None, res_scale=1.0, out_dtype=F32, tm=1088, tn=512,
            k_blocks=1, k_index=0, name="matmul"):
    t = x.shape[0]
    n = w.shape[2]
    k = x.shape[1] // k_blocks
    tm = _tile(t, tm, BF16_SUBLANES)
    tn = _tile(n, tn, LANES)
    in_specs = [pl.BlockSpec((tm, k), lambda i, j: (i, k_index)),
                pl.BlockSpec((None, k, tn), lambda i, j: (layer, k_index, j))]
    args = [x, w]
    if bias is not None:
        in_specs.append(pl.BlockSpec((None, 1, tn), lambda i, j: (layer, 0, j)))
        args.append(bias)
    if res is not None:
        in_specs.append(pl.BlockSpec((tm, tn), lambda i, j: (i, j)))
        args.append(res)
    return pl.pallas_call(
        functools.partial(_mm_kernel, has_bias=bias is not None, has_res=res is not None,
                          res_scale=res_scale),
        grid=(t // tm, n // tn),
        in_specs=in_specs,
        out_specs=pl.BlockSpec((tm, tn), lambda i, j: (i, j)),
        out_shape=jax.ShapeDtypeStruct((t, n), out_dtype),
        compiler_params=_params(2),
        name=name,
    )(*args)


def _mm_nt_kernel(x_ref, a_ref, *rest, shift):
    o_ref, b_ref = rest[-1], rest[-2]
    w = a_ref[...].astype(BF16)
    if shift:
        w = jnp.concatenate([w[shift:], rest[0][...].astype(BF16)], axis=0)
    acc = lax.dot_general(x_ref[...], w, (((1,), (1,)), ((), ())), preferred_element_type=F32)
    o_ref[...] = (acc + b_ref[...]).astype(o_ref.dtype)


def _matmul_nt(x, wt, layer, row0, n, bias, out_dtype=F32, tm=2176, tn=512, name="matmul_nt"):
    t, k = x.shape
    tm = _tile(t, tm, BF16_SUBLANES)
    tn = _tile(n, tn, LANES)
    shift = row0 % tn
    base = row0 - shift
    assert shift % BF16_SUBLANES == 0 and (shift == 0 or tn % shift == 0)
    in_specs = [pl.BlockSpec((tm, k), lambda i, j: (i, 0), pipeline_mode=pl.Buffered(1)),
                pl.BlockSpec((None, tn, k), lambda i, j: (layer, base // tn + j, 0))]
    args = [x, wt]
    if shift:
        in_specs.append(pl.BlockSpec((None, shift, k),
                                     lambda i, j: (layer, (base + (j + 1) * tn) // shift, 0)))
        args.append(wt)
    in_specs.append(pl.BlockSpec((None, 1, tn), lambda i, j: (layer, 0, j)))
    args.append(bias)
    return pl.pallas_call(
        functools.partial(_mm_nt_kernel, shift=shift),
        grid=(t // tm, n // tn),
        in_specs=in_specs,
        out_specs=pl.BlockSpec((tm, tn), lambda i, j: (i, j)),
        out_shape=jax.ShapeDtypeStruct((t, n), out_dtype),
        compiler_params=_params(2),
        name=name,
    )(*args)


def _ln_kernel(r_ref, g_ref, b_ref, of_ref, *maybe_bf16_ref):
    r = r_ref[...]
    mu = jnp.mean(r, axis=1, keepdims=True)
    d = r - mu
    var = jnp.mean(d * d, axis=1, keepdims=True)
    y = d * lax.rsqrt(var + LN_EPS) * g_ref[...] + b_ref[...]
    of_ref[...] = y
    for ob_ref in maybe_bf16_ref:
        ob_ref[...] = y.astype(BF16)


def _layernorm(r, g, b, layer, row0=0, n_rows=None, with_bf16=True):
    d = r.shape[1]
    n_rows = r.shape[0] - row0 if n_rows is None else n_rows
    tr = BF16_SUBLANES
    for cand in range(BF16_SUBLANES, 272 + 1, BF16_SUBLANES):
        if row0 % cand == 0 and n_rows % cand == 0:
            tr = cand
    blk0 = row0 // tr
    vec = pl.BlockSpec((None, 1, d), lambda i: (layer, 0, 0))
    out_row = pl.BlockSpec((tr, d), lambda i: (i, 0))
    out_specs = [out_row]
    out_shape = [jax.ShapeDtypeStruct((n_rows, d), F32)]
    if with_bf16:
        out_specs.append(out_row)
        out_shape.append(jax.ShapeDtypeStruct((n_rows, d), BF16))
    return pl.pallas_call(
        _ln_kernel,
        grid=(n_rows // tr,),
        in_specs=[pl.BlockSpec((tr, d), lambda i: (blk0 + i, 0)), vec, vec],
        out_specs=out_specs,
        out_shape=out_shape,
        compiler_params=_params(1),
        name="layernorm",
    )(r, g, b)


def _ffn_up_kernel(x_ref, wg_ref, wu_ref, o_ref):
    x = x_ref[...]
    a = jnp.dot(x, wg_ref[...].astype(BF16), preferred_element_type=F32)
    u = jnp.dot(x, wu_ref[...].astype(BF16), preferred_element_type=F32)
    o_ref[...] = (a * jax.nn.sigmoid(a) * u).astype(o_ref.dtype)


def _ffn_up(x, wg, wu, layer, tm=2176, tn=512):
    t, k = x.shape
    n = wg.shape[2]
    tm = _tile(t, tm, BF16_SUBLANES)
    tn = _tile(n, tn, LANES)
    wspec = pl.BlockSpec((None, k, tn), lambda i, j: (layer, 0, j))
    return pl.pallas_call(
        _ffn_up_kernel,
        grid=(t // tm, n // tn),
        in_specs=[pl.BlockSpec((tm, k), lambda i, j: (i, 0), pipeline_mode=pl.Buffered(1)),
                  wspec, wspec],
        out_specs=pl.BlockSpec((tm, tn), lambda i, j: (i, j)),
        out_shape=jax.ShapeDtypeStruct((t, n), BF16),
        compiler_params=_params(2),
        name="ffn_up",
    )(x, wg, wu)


def _merge_kernel(hm_ref, zp_ref, zs_ref, wm_ref, wp_ref, ws_ref, gm_ref, gp_ref, gs_ref, o_ref):
    ym = jnp.dot(hm_ref[...], wm_ref[...], preferred_element_type=F32)
    yp = jnp.dot(zp_ref[...], wp_ref[...], preferred_element_type=F32)
    ys = jnp.dot(zs_ref[...], ws_ref[...], preferred_element_type=F32)
    mixed = (jax.nn.sigmoid(gm_ref[...]) * ym + jax.nn.sigmoid(gp_ref[...]) * yp
             + jax.nn.sigmoid(gs_ref[...]) * ys)
    o_ref[...] = mixed.astype(o_ref.dtype)


def _merge(hm, zp, zs, wm, wp, ws, layer, tail, gate_col0, tm=544, tn=512):
    t = hm.shape[0]
    d = wm.shape[2]
    tm = _tile(t, tm, BF16_SUBLANES)
    tn = _tile(d, tn, LANES)
    g0 = gate_col0 // tn
    gd = d // tn

    def xspec(a):
        return pl.BlockSpec((tm, a.shape[1]), lambda i, j: (i, 0))

    def wspec(a):
        return pl.BlockSpec((None, a.shape[1], tn), lambda i, j: (layer, 0, j))

    def gspec(branch):
        return pl.BlockSpec((tm, tn), lambda i, j: (i, g0 + branch * gd + j))

    return pl.pallas_call(
        _merge_kernel,
        grid=(t // tm, d // tn),
        in_specs=[xspec(hm), xspec(zp), xspec(zs), wspec(wm), wspec(wp), wspec(ws),
                  gspec(0), gspec(1), gspec(2)],
        out_specs=pl.BlockSpec((tm, tn), lambda i, j: (i, j)),
        out_shape=jax.ShapeDtypeStruct((t, d), BF16),
        compiler_params=_params(2),
        name="merge",
    )(hm, zp, zs, wm, wp, ws, tail, tail, tail)


def _pick_lane(x, lane_index):
    lane = lax.broadcasted_iota(jnp.int32, x.shape, 1)
    return jnp.sum(jnp.where(lane == lane_index, x, 0.0), axis=1, keepdims=True)


def _col_to_row(col, eye):
    return jnp.sum(jnp.where(eye, col, 0.0), axis=0, keepdims=True)


def _segment_cumsum(x, seg_pos, seg_len):
    s = 1
    while s < seg_len:
        x = x + jnp.where(seg_pos >= s, pltpu.roll(x, s, axis=0), 0.0)
        s *= 2
    return x


def _head_norm_gate(hh, o_pre, mh_g):
    mu = jnp.mean(hh, axis=1, keepdims=True)
    d = hh - mu
    var = jnp.mean(d * d, axis=1, keepdims=True)
    return (jax.nn.sigmoid(o_pre.astype(F32)) * (d * lax.rsqrt(var + LN_EPS) * mh_g)).astype(BF16)


def _mlstm_prompt_kernel(q_ref, k_ref, v_ref, o_ref, g_ref, mhg_ref,
                         hm_ref, c_out, n_out, m_out,
                         ct_s, n_s, m_s, *, n_heads, dqk, dv, scale):
    c = pl.program_id(1)
    last = pl.num_programs(1) - 1
    L = q_ref.shape[0]

    @pl.when(c == 0)
    def _():
        ct_s[...] = jnp.zeros_like(ct_s)
        n_s[...] = jnp.zeros_like(n_s)
        m_s[...] = jnp.zeros_like(m_s)

    g = g_ref[...]
    row = lax.broadcasted_iota(jnp.int32, g.shape, 0)
    lane = lax.broadcasted_iota(jnp.int32, g.shape, 1)
    bsum = _segment_cumsum(_log_sigmoid(g), row, L)
    gb = jnp.where(lane < n_heads, g, bsum)
    gb_t = gb.T
    r2 = lax.broadcasted_iota(jnp.int32, (L, L), 0)
    c2 = lax.broadcasted_iota(jnp.int32, (L, L), 1)
    causal = r2 >= c2

    for h in range(n_heads):
        logi_col = gb[:, h:h + 1]
        b_col = gb[:, n_heads + h:n_heads + h + 1]
        logi_row = gb_t[h:h + 1, :]
        b_row = gb_t[n_heads + h:n_heads + h + 1, :]
        m_prev = m_s[h:h + 1, :]
        dmat = jnp.where(causal, b_col - b_row + logi_row, -jnp.inf)
        inter = b_col + m_prev
        m_t = jnp.maximum(inter, jnp.max(dmat, axis=1, keepdims=True))
        w_inter = jnp.exp(inter - m_t) * scale

        q = q_ref[:, h * dqk:(h + 1) * dqk]
        k = k_ref[:, h * dqk:(h + 1) * dqk]
        v = v_ref[:, h * dv:(h + 1) * dv]
        ct = ct_s[h]
        n_prev = n_s[h:h + 1, :]
        qk = lax.dot_general(q, k, (((1,), (1,)), ((), ())), preferred_element_type=F32) * scale
        s = qk * jnp.exp(dmat - m_t)
        num = (w_inter * jnp.dot(q, ct.astype(BF16), preferred_element_type=F32)
               + jnp.dot(s.astype(BF16), v, preferred_element_type=F32))
        qn = jnp.sum(q.astype(F32) * n_prev, axis=1, keepdims=True)
        den = w_inter * qn + jnp.sum(s, axis=1, keepdims=True)
        hh = num * (1.0 / jnp.maximum(jnp.abs(den), jnp.exp(-m_t)))
        hm_ref[:, h * dv:(h + 1) * dv] = _head_norm_gate(
            hh, o_ref[:, h * dv:(h + 1) * dv], mhg_ref[:, h * dv:(h + 1) * dv])

        m_new = m_t[L - 1:L, :]
        w_c = jnp.exp(inter[L - 1:L, :] - m_new)
        w_s = jnp.exp(b_col[L - 1:L, :] - b_col + logi_col - m_new)
        vs = (w_s * v.astype(F32)).astype(BF16)
        ct_new = w_c * ct + lax.dot_general(k, vs, (((0,), (0,)), ((), ())),
                                            preferred_element_type=F32)
        n_new = w_c * n_prev + jnp.sum(w_s * k.astype(F32), axis=0, keepdims=True)
        ct_s[h] = ct_new
        n_s[h:h + 1, :] = n_new
        m_s[h:h + 1, :] = m_new

    @pl.when(c == last)
    def _():
        for h in range(n_heads):
            c_out[0, h] = ct_s[h].T
        n_out[0] = n_s[...]
        m_out[0] = m_s[...]


def _mlstm_prompt(qkvo, gates, mh_g, layer, batch, seq, n_rows, n_heads, dqk, dv):
    L = PROMPT_CHUNK if seq % PROMPT_CHUNK == 0 else seq
    nc = seq // L
    qw, vw = n_heads * dqk, n_heads * dv
    assert qw % LANES == 0 and 2 * qw % vw == 0

    def rows(b, c):
        return b * nc + c

    return pl.pallas_call(
        functools.partial(_mlstm_prompt_kernel, n_heads=n_heads, dqk=dqk, dv=dv,
                          scale=dqk ** -0.5),
        grid=(batch, nc),
        in_specs=[
            pl.BlockSpec((L, qw), lambda b, c: (rows(b, c), 0)),
            pl.BlockSpec((L, qw), lambda b, c: (rows(b, c), 1)),
            pl.BlockSpec((L, vw), lambda b, c: (rows(b, c), 2 * qw // vw)),
            pl.BlockSpec((L, vw), lambda b, c: (rows(b, c), 2 * qw // vw + 1)),
            pl.BlockSpec((L, LANES), lambda b, c: (rows(b, c), 0)),
            pl.BlockSpec((None, 1, vw), lambda b, c: (layer, 0, 0)),
        ],
        out_specs=[
            pl.BlockSpec((L, vw), lambda b, c: (rows(b, c), 0)),
            pl.BlockSpec((1, n_heads, dv, dqk), lambda b, c: (b, 0, 0, 0)),
            pl.BlockSpec((1, n_heads, dqk), lambda b, c: (b, 0, 0)),
            pl.BlockSpec((1, n_heads, 1), lambda b, c: (b, 0, 0)),
        ],
        out_shape=[
            jax.ShapeDtypeStruct((n_rows, vw), BF16),
            jax.ShapeDtypeStruct((batch, n_heads, dv, dqk), F32),
            jax.ShapeDtypeStruct((batch, n_heads, dqk), F32),
            jax.ShapeDtypeStruct((batch, n_heads, 1), F32),
        ],
        scratch_shapes=[pltpu.VMEM((n_heads, dqk, dv), F32), pltpu.VMEM((n_heads, dqk), F32),
                        pltpu.VMEM((n_heads, 1), F32)],
        compiler_params=_params(2),
        name="mlstm_prompt",
    )(qkvo, qkvo, qkvo, qkvo, gates, mh_g)


def _mlstm_sample_kernel(q_ref, k_ref, v_ref, o_ref, g_ref, mhg_ref, c_ref, n_ref, mrep_ref,
                         *rest, n_heads, scale, seq_len):
    hm_ref, c_out, n_out, m_out = rest[-4:]
    h = pl.program_id(1)
    R = q_ref.shape[0]
    nseq = R // seq_len

    g = g_ref[...]
    row = lax.broadcasted_iota(jnp.int32, g.shape, 0)
    bsum = _segment_cumsum(_log_sigmoid(g), row % seq_len, seq_len)
    logi_col = _pick_lane(g, h)
    b_col = _pick_lane(bsum, h + n_heads)
    m_col = _pick_lane(mrep_ref[...], h)

    r2 = lax.broadcasted_iota(jnp.int32, (R, R), 0)
    c2 = lax.broadcasted_iota(jnp.int32, (R, R), 1)
    eye = r2 == c2
    same = (r2 // seq_len) == (c2 // seq_len)
    b_row = _col_to_row(b_col, eye)
    logi_row = _col_to_row(logi_col, eye)
    dmat = jnp.where(same & (c2 <= r2), b_col - b_row + logi_row, -jnp.inf)
    inter = b_col + m_col
    m_t = jnp.maximum(inter, jnp.max(dmat, axis=1, keepdims=True))
    w_inter = jnp.exp(inter - m_t) * scale

    q = q_ref[...]
    k = k_ref[...]
    v = v_ref[...]
    q32 = q.astype(F32)
    qk = lax.dot_general(q, k, (((1,), (1,)), ((), ())), preferred_element_type=F32) * scale
    s = qk * jnp.exp(dmat - m_t)
    num_s = jnp.dot(s.astype(BF16), v, preferred_element_type=F32)

    seq_of_row = lax.broadcasted_iota(jnp.int32, (R, 1), 0) // seq_len
    num_c = jnp.zeros(num_s.shape, F32)
    qn = jnp.zeros((R, 1), F32)
    for j in range(nseq):
        mine = seq_of_row == j
        cq = lax.dot_general(q, c_ref[j, 0].astype(BF16), (((1,), (1,)), ((), ())),
                             preferred_element_type=F32)
        num_c = jnp.where(mine, cq, num_c)
        qn = jnp.where(mine, jnp.sum(q32 * n_ref[j, 0], axis=1, keepdims=True), qn)
    num = w_inter * num_c + num_s
    den = w_inter * qn + jnp.sum(s, axis=1, keepdims=True)
    hh = num * (1.0 / jnp.maximum(jnp.abs(den), jnp.exp(-m_t)))
    hm_ref[...] = _head_norm_gate(hh, o_ref[...], mhg_ref[...])

    pick_last = same & (c2 % seq_len == seq_len - 1)

    def last_of_seq(col):
        return jnp.sum(jnp.where(pick_last, _col_to_row(col, eye), 0.0), axis=1, keepdims=True)

    m_new = last_of_seq(m_t)
    w_c = jnp.exp(last_of_seq(inter) - m_new)
    w_s = jnp.exp(last_of_seq(b_col) - b_col + logi_col - m_new)
    vs = (w_s * v.astype(F32)).astype(BF16)
    kw = w_s * k.astype(F32)
    for j in range(nseq):
        mine = seq_of_row == j
        upd = lax.dot_general(jnp.where(mine, vs, jnp.zeros_like(vs)), k,
                              (((0,), (0,)), ((), ())), preferred_element_type=F32)
        wc_j = w_c[j * seq_len:j * seq_len + 1, :]
        c_out[j, 0] = wc_j * c_ref[j, 0] + upd
        n_out[j, 0] = wc_j * n_ref[j, 0] + jnp.sum(jnp.where(mine, kw, 0.0), axis=0, keepdims=True)
    m_out[0] = m_new


def _mlstm_sample(qkvo, gates, mh_g, state_c, state_n, m_rep, hm, c_prev, layer, row0,
                  n_seq, seq_len, n_heads, dqk, dv):
    nb = _tile(n_seq, SAMPLE_SEQS, 1)
    R = nb * seq_len
    assert row0 % R == 0
    blk0 = row0 // R
    qw, vw = n_heads * dqk, n_heads * dv
    o_blk0 = (2 * qw + vw) // dv
    ts = n_seq * seq_len
    any_spec = pl.BlockSpec(memory_space=pl.ANY)

    in_specs = [
        pl.BlockSpec((R, dqk), lambda i, h: (blk0 + i, h)),
        pl.BlockSpec((R, dqk), lambda i, h: (blk0 + i, n_heads + h)),
        pl.BlockSpec((R, dv), lambda i, h: (blk0 + i, 2 * qw // dv + h)),
        pl.BlockSpec((R, dv), lambda i, h: (blk0 + i, o_blk0 + h)),
        pl.BlockSpec((R, LANES), lambda i, h: (blk0 + i, 0)),
        pl.BlockSpec((None, 1, dv), lambda i, h: (layer, 0, h)),
        pl.BlockSpec((None, nb, 1, dv, dqk), lambda i, h: (layer, i, h, 0, 0)),
        pl.BlockSpec((None, nb, 1, 1, dqk), lambda i, h: (layer, i, h, 0, 0)),
        pl.BlockSpec((R, n_heads), lambda i, h: (i, 0)),
        any_spec,
    ]
    args = [qkvo, qkvo, qkvo, qkvo, gates, mh_g, state_c, state_n, m_rep, hm]
    aliases = {len(args) - 1: 0}
    if c_prev is not None:
        in_specs.append(any_spec)
        args.append(c_prev)
        aliases[len(args) - 1] = 1

    return pl.pallas_call(
        functools.partial(_mlstm_sample_kernel, n_heads=n_heads, scale=dqk ** -0.5,
                          seq_len=seq_len),
        grid=(n_seq // nb, n_heads),
        in_specs=in_specs,
        out_specs=[
            pl.BlockSpec((R, dv), lambda i, h: (blk0 + i, h)),
            pl.BlockSpec((None, nb, 1, dv, dqk), lambda i, h: (layer, i, h, 0, 0)),
            pl.BlockSpec((nb, 1, 1, dqk), lambda i, h: (i, h, 0, 0)),
            pl.BlockSpec((1, R, 1), lambda i, h: (h, i, 0)),
        ],
        out_shape=[
            jax.ShapeDtypeStruct(hm.shape, hm.dtype),
            jax.ShapeDtypeStruct(state_c.shape, F32),
            jax.ShapeDtypeStruct(state_n.shape[1:], F32),
            jax.ShapeDtypeStruct((n_heads, ts, 1), F32),
        ],
        input_output_aliases=aliases,
        compiler_params=_params(2),
        name="mlstm_sample",
    )(*args)


def _pool_conv_prompt_kernel(u_ref, uh_ref, cb_ref, cc_ref, cx_ref, cch_ref, cxh_ref,
                             maps_ref, scale_ref, cw_ref,
                             zp_ref, zs_ref, ps_ref, cs_ref, full_s, pf_s, *, windows):
    t = pl.program_id(1)
    last = pl.num_programs(1) - 1
    tr = u_ref.shape[0]
    gw = maps_ref.shape[1]
    n_buf = ps_ref.shape[1]
    taps = cw_ref.shape[0]
    has_past = t > 0

    u = u_ref[...]
    full_s[0:POOL_HALO, :] = jnp.where(has_past, uh_ref[...], 0.0)
    full_s[POOL_HALO:, :] = u
    pos = t * tr + lax.broadcasted_iota(jnp.int32, (tr, 1), 0)
    for g, w in enumerate(windows):
        cols = slice(g * gw, (g + 1) * gw)
        acc = u[:, cols]
        for j in range(1, w):
            acc = acc + full_s[POOL_HALO - j:POOL_HALO - j + tr, cols]
        cnt = jnp.minimum(pos + 1, w).astype(F32)
        zf = acc / cnt - u[:, cols]
        zp = jnp.dot(zf.astype(BF16), maps_ref[g], preferred_element_type=F32) * scale_ref[:, cols]
        zp_ref[:, cols] = zp.astype(BF16)

    p = cc_ref[...] * cx_ref[...]
    pf_s[0:CONV_HALO, :] = jnp.where(has_past, cch_ref[...] * cxh_ref[...], 0.0)
    pf_s[CONV_HALO:, :] = p
    y = p * cw_ref[taps - 1:taps, :]
    for j in range(taps - 1):
        back = taps - 1 - j
        y = y + pf_s[CONV_HALO - back:CONV_HALO - back + tr, :] * cw_ref[j:j + 1, :]
    zs_ref[...] = (cb_ref[...] * y).astype(BF16)

    @pl.when(t == last)
    def _():
        ps_ref[0] = full_s[POOL_HALO + tr - n_buf:POOL_HALO + tr, :]
        cs_ref[0] = pf_s[CONV_HALO + tr - (taps - 1):CONV_HALO + tr, :]


def _pool_conv_prompt(tail, maps, pscale, conv_w, layer, batch, seq, width, n_buf):
    tr = _tile(seq, 256, POOL_HALO)
    nt = seq // tr
    taps = conv_w.shape[1]
    n_groups, gw = maps.shape[1], maps.shape[2]
    hp = tr // POOL_HALO
    hc = tr // CONV_HALO

    def rows(b, t):
        return b * nt + t

    def main(col):
        return pl.BlockSpec((tr, width), lambda b, t: (rows(b, t), col))

    def halo(col, h, per_tile):
        return pl.BlockSpec((h, width), lambda b, t: (jnp.maximum(rows(b, t) * per_tile - 1, 0), col))

    return pl.pallas_call(
        functools.partial(_pool_conv_prompt_kernel, windows=POOL_WINDOWS),
        grid=(batch, nt),
        in_specs=[main(0), halo(0, POOL_HALO, hp), main(1), main(2), main(3),
                  halo(2, CONV_HALO, hc), halo(3, CONV_HALO, hc),
                  pl.BlockSpec((None, n_groups, gw, gw), lambda b, t: (layer, 0, 0, 0)),
                  pl.BlockSpec((None, 1, width), lambda b, t: (layer, 0, 0)),
                  pl.BlockSpec((None, taps, width), lambda b, t: (layer, 0, 0))],
        out_specs=[pl.BlockSpec((tr, width), lambda b, t: (rows(b, t), 0)),
                   pl.BlockSpec((tr, width), lambda b, t: (rows(b, t), 0)),
                   pl.BlockSpec((1, n_buf, width), lambda b, t: (b, 0, 0)),
                   pl.BlockSpec((1, taps - 1, width), lambda b, t: (b, 0, 0))],
        out_shape=[jax.ShapeDtypeStruct((tail.shape[0], width), BF16),
                   jax.ShapeDtypeStruct((tail.shape[0], width), BF16),
                   jax.ShapeDtypeStruct((batch, n_buf, width), F32),
                   jax.ShapeDtypeStruct((batch, taps - 1, width), F32)],
        scratch_shapes=[pltpu.VMEM((POOL_HALO + tr, width), F32),
                        pltpu.VMEM((CONV_HALO + tr, width), F32)],
        compiler_params=_params(2),
        name="pool_conv_prompt",
    )(tail, tail, tail, tail, tail, tail, tail, maps, pscale, conv_w)


def _pool_conv_sample_kernel(x_ref, ps_ref, cs_ref, maps_ref, scale_ref, cw_ref,
                             zp_ref, zs_ref, pn_ref, cn_ref, *, windows, start_pos):
    g = pl.program_id(0)
    seq_len = x_ref.shape[1]
    n_buf = ps_ref.shape[0]
    taps = cw_ref.shape[0]
    win = jnp.int32(windows[0])
    for i, w in enumerate(windows):
        win = jnp.where(g == i, jnp.int32(w), win)

    hist = [ps_ref[j] for j in range(n_buf)] + [x_ref[0, t] for t in range(seq_len)]
    for t in range(seq_len):
        cur = n_buf + t
        acc = hist[cur]
        for j in range(1, max(windows)):
            acc = acc + jnp.where(j < win, hist[cur - j], 0.0)
        cnt = jnp.minimum(start_pos + t + 1, win).astype(F32)
        zf = acc / cnt - hist[cur]
        zp = jnp.dot(zf.astype(BF16), maps_ref[0], preferred_element_type=F32) * scale_ref[...]
        zp_ref[t] = zp.astype(BF16)
    for j in range(n_buf):
        pn_ref[j] = hist[seq_len + j]

    prod = [cs_ref[j] for j in range(taps - 1)] + [x_ref[2, t] * x_ref[3, t] for t in range(seq_len)]
    for t in range(seq_len):
        y = prod[t] * cw_ref[0:1, :]
        for j in range(1, taps):
            y = y + prod[t + j] * cw_ref[j:j + 1, :]
        zs_ref[t] = (x_ref[1, t] * y).astype(BF16)
    for j in range(taps - 1):
        cn_ref[j] = prod[seq_len + j]


def _pool_conv_sample(x4, pstate_t, cstate_t, maps, pscale, conv_w, layer):
    _, seq_len, n_seq, width = x4.shape
    n_buf = pstate_t.shape[0]
    taps = conv_w.shape[1]
    n_groups, gw = maps.shape[1], maps.shape[2]
    assert len(POOL_WINDOWS) == n_groups and n_buf >= max(POOL_WINDOWS) - 1

    return pl.pallas_call(
        functools.partial(_pool_conv_sample_kernel, windows=POOL_WINDOWS, start_pos=PAST_LEN),
        grid=(n_groups,),
        in_specs=[pl.BlockSpec((4, seq_len, n_seq, gw), lambda g: (0, 0, 0, g)),
                  pl.BlockSpec((n_buf, n_seq, gw), lambda g: (0, 0, g)),
                  pl.BlockSpec((taps - 1, n_seq, gw), lambda g: (0, 0, g)),
                  pl.BlockSpec((None, 1, gw, gw), lambda g: (layer, g, 0, 0)),
                  pl.BlockSpec((None, 1, gw), lambda g: (layer, 0, g)),
                  pl.BlockSpec((None, taps, gw), lambda g: (layer, 0, g))],
        out_specs=[pl.BlockSpec((seq_len, n_seq, gw), lambda g: (0, 0, g)),
                   pl.BlockSpec((seq_len, n_seq, gw), lambda g: (0, 0, g)),
                   pl.BlockSpec((n_buf, n_seq, gw), lambda g: (0, 0, g)),
                   pl.BlockSpec((taps - 1, n_seq, gw), lambda g: (0, 0, g))],
        out_shape=[jax.ShapeDtypeStruct((seq_len, n_seq, width), BF16),
                   jax.ShapeDtypeStruct((seq_len, n_seq, width), BF16),
                   jax.ShapeDtypeStruct(pstate_t.shape, F32),
                   jax.ShapeDtypeStruct(cstate_t.shape, F32)],
        compiler_params=_params(1),
        name="pool_conv_sample",
    )(x4, pstate_t, cstate_t, maps, pscale, conv_w)


def kernel(x_prompt, x_sample, state_C, state_n, state_m, state_pool, state_conv, w_in, b_in, mh_g, pool_maps, pool_scale, conv_w, w_bm, w_bp, w_bs, w_o, ln1_g, ln1_b, w_gate, w_up, w_down, ln2_g, ln2_b):
    depth = w_in.shape[0]
    batch, seq, d = x_prompt.shape
    n_seq, seq_len, _ = x_sample.shape
    n_heads, dv, dqk = state_C.shape[2], state_C.shape[3], state_C.shape[4]
    n_buf, pw = state_pool.shape[2], state_pool.shape[3]
    cw = conv_w.shape[2]
    dff = w_gate.shape[2]
    assert pw == cw and dv == 2 * dqk and 2 * n_heads <= LANES
    alpha = (2 * depth) ** 0.25
    tp, ts = batch * seq, n_seq * seq_len
    t_all = tp + ts

    off_i = 2 * n_heads * dqk + 2 * n_heads * dv
    off_p = off_i + 2 * n_heads
    gate_col0 = pw + 3 * cw
    k_blocks = 2 if dff % (2 * LANES) == 0 else 1

    w_in_t = jnp.swapaxes(w_in, 1, 2)
    w_if = jnp.pad(w_in[:, :, off_i:off_p], ((0, 0), (0, 0), (0, LANES - (off_p - off_i)))).astype(BF16)
    b3 = b_in[:, None, :]
    b_qkvo, b_tail = b3[:, :, :off_i], b3[:, :, off_p:]
    b_if = jnp.pad(b3[:, :, off_i:off_p], ((0, 0), (0, 0), (0, LANES - (off_p - off_i))))
    wbm, wbp, wbs = (_to_bf16(w) for w in (w_bm, w_bp, w_bs))
    maps = pool_maps.astype(BF16)
    mhg3, pscale3 = mh_g[:, None, :], pool_scale[:, None, :]
    ln1g, ln1b, ln2g, ln2b = (a[:, None, :] for a in (ln1_g, ln1_b, ln2_g, ln2_b))
    state_n5 = state_n[:, :, :, None, :]
    pool_t = state_pool.transpose(0, 2, 1, 3)
    conv_t = state_conv.transpose(0, 2, 1, 3)

    x = jnp.concatenate([x_prompt.reshape(tp, d), x_sample.reshape(ts, d)], axis=0)
    xb = x.astype(BF16)

    outs = [[] for _ in range(9)]
    c_s = None
    for l in range(depth):
        qkvo = _matmul_nt(xb, w_in_t, l, 0, off_i, b_qkvo, out_dtype=BF16, name="in_qkvo")
        gates = _matmul(xb, w_if, l, bias=b_if, name="in_if")
        tail = _matmul_nt(xb, w_in_t, l, off_p, w_in.shape[2] - off_p, b_tail, name="in_tail")

        hm, c_p, n_p, m_p = _mlstm_prompt(qkvo, gates, mhg3, l, batch, seq, t_all, n_heads, dqk, dv)
        m_rep = jnp.repeat(state_m[l], seq_len, axis=0)
        hm, c_s, n_s, m_s = _mlstm_sample(qkvo, gates, mhg3, state_C, state_n5, m_rep, hm, c_s, l,
                                          tp, n_seq, seq_len, n_heads, dqk, dv)

        zp, zs, ps_p, cs_p = _pool_conv_prompt(tail, maps, pscale3, conv_w, l, batch, seq, pw, n_buf)
        x4 = tail[tp:, :pw + 3 * cw].reshape(n_seq, seq_len, 4, pw).transpose(2, 1, 0, 3)
        zp_s, zs_s, ps_s, cs_s = _pool_conv_sample(x4, pool_t[l], conv_t[l], maps, pscale3, conv_w, l)
        zp = lax.dynamic_update_slice(zp, zp_s.transpose(1, 0, 2).reshape(ts, pw), (tp, 0))
        zs = lax.dynamic_update_slice(zs, zs_s.transpose(1, 0, 2).reshape(ts, cw), (tp, 0))

        mixed = _merge(hm, zp, zs, wbm, wbp, wbs, l, tail, gate_col0)
        r1 = _matmul(mixed, w_o, l, res=x, res_scale=alpha, name="out_proj")
        x1, x1b = _layernorm(r1, ln1g, ln1b, l)

        ff = _ffn_up(x1b, w_gate, w_up, l, tn=FF_TILE)
        r2 = _matmul(ff, w_down, l, res=x1, res_scale=alpha, tn=FF_TILE, k_blocks=k_blocks,
                     name="ffn_down")
        for kb in range(1, k_blocks):
            r2 = _matmul(ff, w_down, l, res=r2, tn=FF_TILE, k_blocks=k_blocks, k_index=kb,
                         name="ffn_down")
        if l + 1 < depth:
            x, xb = _layernorm(r2, ln2g, ln2b, l)
        else:
            y_p, = _layernorm(r2, ln2g, ln2b, l, 0, tp, with_bf16=False)
            y_s, = _layernorm(r2, ln2g, ln2b, l, tp, ts, with_bf16=False)

        new = (c_p, n_p, m_p[:, :, 0], ps_p, cs_p,
               n_s[:, :, 0, :], m_s[:, seq_len - 1::seq_len, 0].T,
               ps_s.transpose(1, 0, 2), cs_s.transpose(1, 0, 2))
        for acc, val in zip(outs, new):
            acc.append(val)

    st = [jnp.stack(o) for o in outs]
    return (y_p.reshape(batch, seq, d), y_s.reshape(n_seq, seq_len, d),
            st[0], st[1], st[2], st[3], st[4], c_s, st[5], st[6], st[7], st[8])
```

```python
import functools

import jax
import jax.numpy as jnp
from jax import lax
from jax.experimental import pallas as pl
from jax.experimental.pallas import tpu as pltpu

F32 = jnp.float32
BF16 = jnp.bfloat16

POOL_WINDOWS = (2, 4, 8, 16)
PAST_LEN = 16384
LN_EPS = 1e-5
PROMPT_CHUNK = 128

LANES = 128
BF16_SUBLANES = 16
VMEM_LIMIT = 56 * 1024 * 1024
POOL_HALO = 16
CONV_HALO = 8
SAMPLE_SEQS = 16
FF_TILE = 256


def _params(n_axes):
    return pltpu.CompilerParams(dimension_semantics=("arbitrary",) * n_axes,
                                vmem_limit_bytes=VMEM_LIMIT)


def _tile(dim, target, mult):
    best = None
    for d in range(mult, min(dim, target) + 1, mult):
        if dim % d == 0:
            best = d
    return dim if best is None else best


def _log_sigmoid(x):
    return jnp.minimum(x, 0.0) - jnp.log(1.0 + jnp.exp(-jnp.abs(x)))


def _cast_kernel(x_ref, o_ref):
    o_ref[...] = x_ref[...].astype(o_ref.dtype)


def _to_bf16(w):
    layers, rows, cols = w.shape
    tc = _tile(cols, 4096, LANES)
    tr = _tile(rows, max(BF16_SUBLANES, 1024 * 1024 // tc), BF16_SUBLANES)
    spec = pl.BlockSpec((None, tr, tc), lambda l, i, j: (l, i, j))
    return pl.pallas_call(
        _cast_kernel,
        grid=(layers, rows // tr, cols // tc),
        in_specs=[spec],
        out_specs=spec,
        out_shape=jax.ShapeDtypeStruct((layers, rows, cols), BF16),
        compiler_params=_params(3),
        name="to_bf16",
    )(w)


def _stack_kernel(xp_ref, xs_ref, of_ref, ob_ref, *, prompt_tiles):
    x = jnp.where(pl.program_id(0) < prompt_tiles, xp_ref[...], xs_ref[...])
    of_ref[...] = x
    ob_ref[...] = x.astype(BF16)


def _stack_rows(xp, xs):
    tp, d = xp.shape
    ts = xs.shape[0]
    tr = BF16_SUBLANES
    for cand in range(BF16_SUBLANES, 256 + 1, BF16_SUBLANES):
        if tp % cand == 0 and ts % cand == 0:
            tr = cand
    n_p, n_s = tp // tr, ts // tr
    out = pl.BlockSpec((tr, d), lambda i: (i, 0))
    return pl.pallas_call(
        functools.partial(_stack_kernel, prompt_tiles=n_p),
        grid=(n_p + n_s,),
        in_specs=[pl.BlockSpec((tr, d), lambda i: (jnp.minimum(i, n_p - 1), 0)),
                  pl.BlockSpec((tr, d), lambda i: (jnp.maximum(i - n_p, 0), 0))],
        out_specs=[out, out],
        out_shape=[jax.ShapeDtypeStruct((tp + ts, d), F32),
                   jax.ShapeDtypeStruct((tp + ts, d), BF16)],
        compiler_params=_params(1),
        name="stack_rows",
    )(xp, xs)


def _mm_kernel(*refs, has_bias, has_res, res_scale):
    x_ref, w_ref = refs[0], refs[1]
    o_ref = refs[-1]
    acc = jnp.dot(x_ref[...], w_ref[...].astype(BF16), preferred_element_type=F32)
    pos = 2
    if has_bias:
        acc = acc + refs[pos][...]
        pos += 1
    if has_res:
        acc = acc + res_scale * refs[pos][...]
    o_ref[...] = acc.astype(o_ref.dtype)


def _matmul(x, w, layer, bias=None, res=None, res_scale=1.0, out_dtype=F32, tm=1088, tn=512,
            k_blocks=1, k_index=0, name="matmul"):
    t = x.shape[0]
    n = w.shape[2]
    k = x.shape[1] // k_blocks
    tm = _tile(t, tm, BF16_SUBLANES)
    tn = _tile(n, tn, LANES)
    in_specs = [pl.BlockSpec((tm, k), lambda i, j: (i, k_index)),
                pl.BlockSpec((None, k, tn), lambda i, j: (layer, k_index, j))]
    args = [x, w]
    if bias is not None:
        in_specs.append(pl.BlockSpec((None, 1, tn), lambda i, j: (layer, 0, j)))
        args.append(bias)
    if res is not None:
        in_specs.append(pl.BlockSpec((tm, tn), lambda i, j: (i, j)))
        args.append(res)
    return pl.pallas_call(
        functools.partial(_mm_kernel, has_bias=bias is not None, has_res=res is not None,
                          res_scale=res_scale),
        grid=(t // tm, n // tn),
        in_specs=in_specs,
        out_specs=pl.BlockSpec((tm, tn), lambda i, j: (i, j)),
        out_shape=jax.ShapeDtypeStruct((t, n), out_dtype),
        compiler_params=_params(2),
        name=name,
    )(*args)


def _mm_nt_kernel(x_ref, a_ref, *rest, shift):
    o_ref, b_ref = rest[-1], rest[-2]
    w = a_ref[...].astype(BF16)
    if shift:
        w = jnp.concatenate([w[shift:], rest[0][...].astype(BF16)], axis=0)
    acc = lax.dot_general(x_ref[...], w, (((1,), (1,)), ((), ())), preferred_element_type=F32)
    o_ref[...] = (acc + b_ref[...]).astype(o_ref.dtype)


def _matmul_nt(x, wt, layer, row0, n, bias, out_dtype=F32, tm=2176, tn=512, name="matmul_nt"):
    t, k = x.shape
    tm = _tile(t, tm, BF16_SUBLANES)
    tn = _tile(n, tn, LANES)
    shift = row0 % tn
    base = row0 - shift
    assert shift % BF16_SUBLANES == 0 and (shift == 0 or tn % shift == 0)
    in_specs = [pl.BlockSpec((tm, k), lambda i, j: (i, 0), pipeline_mode=pl.Buffered(1)),
                pl.BlockSpec((None, tn, k), lambda i, j: (layer, base // tn + j, 0))]
    args = [x, wt]
    if shift:
        in_specs.append(pl.BlockSpec((None, shift, k),
                                     lambda i, j: (layer, (base + (j + 1) * tn) // shift, 0)))
        args.append(wt)
    in_specs.append(pl.BlockSpec((None, 1, tn), lambda i, j: (layer, 0, j)))
    args.append(bias)
    return pl.pallas_call(
        functools.partial(_mm_nt_kernel, shift=shift),
        grid=(t // tm, n // tn),
        in_specs=in_specs,
        out_specs=pl.BlockSpec((tm, tn), lambda i, j: (i, j)),
        out_shape=jax.ShapeDtypeStruct((t, n), out_dtype),
        compiler_params=_params(2),
        name=name,
    )(*args)


def _ln_kernel(r_ref, g_ref, b_ref, of_ref, *maybe_bf16_ref):
    r = r_ref[...]
    mu = jnp.mean(r, axis=1, keepdims=True)
    d = r - mu
    var = jnp.mean(d * d, axis=1, keepdims=True)
    y = d * lax.rsqrt(var + LN_EPS) * g_ref[...] + b_ref[...]
    of_ref[...] = y
    for ob_ref in maybe_bf16_ref:
        ob_ref[...] = y.astype(BF16)


def _layernorm(r, g, b, layer, row0=0, n_rows=None, with_bf16=True):
    d = r.shape[1]
    n_rows = r.shape[0] - row0 if n_rows is None else n_rows
    tr = BF16_SUBLANES
    for cand in range(BF16_SUBLANES, 272 + 1, BF16_SUBLANES):
        if row0 % cand == 0 and n_rows % cand == 0:
            tr = cand
    blk0 = row0 // tr
    vec = pl.BlockSpec((None, 1, d), lambda i: (layer, 0, 0))
    out_row = pl.BlockSpec((tr, d), lambda i: (i, 0))
    out_specs = [out_row]
    out_shape = [jax.ShapeDtypeStruct((n_rows, d), F32)]
    if with_bf16:
        out_specs.append(out_row)
        out_shape.append(jax.ShapeDtypeStruct((n_rows, d), BF16))
    return pl.pallas_call(
        _ln_kernel,
        grid=(n_rows // tr,),
        in_specs=[pl.BlockSpec((tr, d), lambda i: (blk0 + i, 0)), vec, vec],
        out_specs=out_specs,
        out_shape=out_shape,
        compiler_params=_params(1),
        name="layernorm",
    )(r, g, b)


def _ffn_up_kernel(x_ref, wg_ref, wu_ref, o_ref):
    x = x_ref[...]
    a = jnp.dot(x, wg_ref[...].astype(BF16), preferred_element_type=F32)
    u = jnp.dot(x, wu_ref[...].astype(BF16), preferred_element_type=F32)
    o_ref[...] = (a * jax.nn.sigmoid(a) * u).astype(o_ref.dtype)


def _ffn_up(x, wg, wu, layer, tm=2176, tn=512):
    t, k = x.shape
    n = wg.shape[2]
    tm = _tile(t, tm, BF16_SUBLANES)
    tn = _tile(n, tn, LANES)
    wspec = pl.BlockSpec((None, k, tn), lambda i, j: (layer, 0, j))
    return pl.pallas_call(
        _ffn_up_kernel,
        grid=(t // tm, n // tn),
        in_specs=[pl.BlockSpec((tm, k), lambda i, j: (i, 0), pipeline_mode=pl.Buffered(1)),
                  wspec, wspec],
        out_specs=pl.BlockSpec((tm, tn), lambda i, j: (i, j)),
        out_shape=jax.ShapeDtypeStruct((t, n), BF16),
        compiler_params=_params(2),
        name="ffn_up",
    )(x, wg, wu)


def _merge_kernel(hm_ref, zp_ref, zs_ref, wm_ref, wp_ref, ws_ref, gm_ref, gp_ref, gs_ref, o_ref):
    ym = jnp.dot(hm_ref[...], wm_ref[...], preferred_element_type=F32)
    yp = jnp.dot(zp_ref[...], wp_ref[...], preferred_element_type=F32)
    ys = jnp.dot(zs_ref[...], ws_ref[...], preferred_element_type=F32)
    mixed = (jax.nn.sigmoid(gm_ref[...]) * ym + jax.nn.sigmoid(gp_ref[...]) * yp
             + jax.nn.sigmoid(gs_ref[...]) * ys)
    o_ref[...] = mixed.astype(o_ref.dtype)


def _merge(hm, zp, zs, wm, wp, ws, layer, tail, gate_col0, tm=1088, tn=256):
    t = hm.shape[0]
    d = wm.shape[2]
    tm = _tile(t, tm, BF16_SUBLANES)
    tn = _tile(d, tn, LANES)
    g0 = gate_col0 // tn
    gd = d // tn

    def xspec(a):
        return pl.BlockSpec((tm, a.shape[1]), lambda i, j: (i, 0))

    def wspec(a):
        return pl.BlockSpec((None, a.shape[1], tn), lambda i, j: (layer, 0, j))

    def gspec(branch):
        return pl.BlockSpec((tm, tn), lambda i, j: (i, g0 + branch * gd + j))

    return pl.pallas_call(
        _merge_kernel,
        grid=(t // tm, d // tn),
        in_specs=[xspec(hm), xspec(zp), xspec(zs), wspec(wm), wspec(wp), wspec(ws),
                  gspec(0), gspec(1), gspec(2)],
        out_specs=pl.BlockSpec((tm, tn), lambda i, j: (i, j)),
        out_shape=jax.ShapeDtypeStruct((t, d), BF16),
        compiler_params=_params(2),
        name="merge",
    )(hm, zp, zs, wm, wp, ws, tail, tail, tail)


def _pick_lane(x, lane_index):
    lane = lax.broadcasted_iota(jnp.int32, x.shape, 1)
    return jnp.sum(jnp.where(lane == lane_index, x, 0.0), axis=1, keepdims=True)


def _col_to_row(col, eye):
    return jnp.sum(jnp.where(eye, col, 0.0), axis=0, keepdims=True)


def _segment_cumsum(x, seg_pos, seg_len):
    s = 1
    while s < seg_len:
        x = x + jnp.where(seg_pos >= s, pltpu.roll(x, s, axis=0), 0.0)
        s *= 2
    return x


def _head_norm_gate(hh, o_pre, mh_g):
    mu = jnp.mean(hh, axis=1, keepdims=True)
    d = hh - mu
    var = jnp.mean(d * d, axis=1, keepdims=True)
    return (jax.nn.sigmoid(o_pre.astype(F32)) * (d * lax.rsqrt(var + LN_EPS) * mh_g)).astype(BF16)


def _mlstm_prompt_kernel(q_ref, k_ref, v_ref, o_ref, g_ref, mhg_ref,
                         hm_ref, c_out, n_out, m_out,
                         ct_s, n_s, m_s, *, n_heads, dqk, dv, scale):
    c = pl.program_id(1)
    last = pl.num_programs(1) - 1
    L = q_ref.shape[0]

    @pl.when(c == 0)
    def _():
        ct_s[...] = jnp.zeros_like(ct_s)
        n_s[...] = jnp.zeros_like(n_s)
        m_s[...] = jnp.zeros_like(m_s)

    g = g_ref[...]
    row = lax.broadcasted_iota(jnp.int32, g.shape, 0)
    lane = lax.broadcasted_iota(jnp.int32, g.shape, 1)
    bsum = _segment_cumsum(_log_sigmoid(g), row, L)
    gb = jnp.where(lane < n_heads, g, bsum)
    gb_t = gb.T
    r2 = lax.broadcasted_iota(jnp.int32, (L, L), 0)
    c2 = lax.broadcasted_iota(jnp.int32, (L, L), 1)
    causal = r2 >= c2

    for h in range(n_heads):
        logi_col = gb[:, h:h + 1]
        b_col = gb[:, n_heads + h:n_heads + h + 1]
        logi_row = gb_t[h:h + 1, :]
        b_row = gb_t[n_heads + h:n_heads + h + 1, :]
        m_prev = m_s[h:h + 1, :]
        dmat = jnp.where(causal, b_col - b_row + logi_row, -jnp.inf)
        inter = b_col + m_prev
        m_t = jnp.maximum(inter, jnp.max(dmat, axis=1, keepdims=True))
        w_inter = jnp.exp(inter - m_t) * scale

        q = q_ref[:, h * dqk:(h + 1) * dqk]
        k = k_ref[:, h * dqk:(h + 1) * dqk]
        v = v_ref[:, h * dv:(h + 1) * dv]
        ct = ct_s[h]
        n_prev = n_s[h:h + 1, :]
        qk = lax.dot_general(q, k, (((1,), (1,)), ((), ())), preferred_element_type=F32) * scale
        s = qk * jnp.exp(dmat - m_t)
        num = (w_inter * jnp.dot(q, ct.astype(BF16), preferred_element_type=F32)
               + jnp.dot(s.astype(BF16), v, preferred_element_type=F32))
        qn = jnp.sum(q.astype(F32) * n_prev, axis=1, keepdims=True)
        den = w_inter * qn + jnp.sum(s, axis=1, keepdims=True)
        hh = num * (1.0 / jnp.maximum(jnp.abs(den), jnp.exp(-m_t)))
        hm_ref[:, h * dv:(h + 1) * dv] = _head_norm_gate(
            hh, o_ref[:, h * dv:(h + 1) * dv], mhg_ref[:, h * dv:(h + 1) * dv])

        m_new = m_t[L - 1:L, :]
        w_c = jnp.exp(inter[L - 1:L, :] - m_new)
        w_s = jnp.exp(b_col[L - 1:L, :] - b_col + logi_col - m_new)
        vs = (w_s * v.astype(F32)).astype(BF16)
        ct_new = w_c * ct + lax.dot_general(k, vs, (((0,), (0,)), ((), ())),
                                            preferred_element_type=F32)
        n_new = w_c * n_prev + jnp.sum(w_s * k.astype(F32), axis=0, keepdims=True)
        ct_s[h] = ct_new
        n_s[h:h + 1, :] = n_new
        m_s[h:h + 1, :] = m_new

    @pl.when(c == last)
    def _():
        for h in range(n_heads):
            c_out[0, h] = ct_s[h].T
        n_out[0] = n_s[...]
        m_out[0] = m_s[...]


def _mlstm_prompt(qkvo, gates, mh_g, layer, batch, seq, n_rows, n_heads, dqk, dv):
    L = PROMPT_CHUNK if seq % PROMPT_CHUNK == 0 else seq
    nc = seq // L
    qw, vw = n_heads * dqk, n_heads * dv
    assert qw % LANES == 0 and 2 * qw % vw == 0

    def rows(b, c):
        return b * nc + c

    return pl.pallas_call(
        functools.partial(_mlstm_prompt_kernel, n_heads=n_heads, dqk=dqk, dv=dv,
                          scale=dqk ** -0.5),
        grid=(batch, nc),
        in_specs=[
            pl.BlockSpec((L, qw), lambda b, c: (rows(b, c), 0)),
            pl.BlockSpec((L, qw), lambda b, c: (rows(b, c), 1)),
            pl.BlockSpec((L, vw), lambda b, c: (rows(b, c), 2 * qw // vw)),
            pl.BlockSpec((L, vw), lambda b, c: (rows(b, c), 2 * qw // vw + 1)),
            pl.BlockSpec((L, LANES), lambda b, c: (rows(b, c), 0)),
            pl.BlockSpec((None, 1, vw), lambda b, c: (layer, 0, 0)),
        ],
        out_specs=[
            pl.BlockSpec((L, vw), lambda b, c: (rows(b, c), 0)),
            pl.BlockSpec((1, n_heads, dv, dqk), lambda b, c: (b, 0, 0, 0)),
            pl.BlockSpec((1, n_heads, dqk), lambda b, c: (b, 0, 0)),
            pl.BlockSpec((1, n_heads, 1), lambda b, c: (b, 0, 0)),
        ],
        out_shape=[
            jax.ShapeDtypeStruct((n_rows, vw), BF16),
            jax.ShapeDtypeStruct((batch, n_heads, dv, dqk), F32),
            jax.ShapeDtypeStruct((batch, n_heads, dqk), F32),
            jax.ShapeDtypeStruct((batch, n_heads, 1), F32),
        ],
        scratch_shapes=[pltpu.VMEM((n_heads, dqk, dv), F32), pltpu.VMEM((n_heads, dqk), F32),
                        pltpu.VMEM((n_heads, 1), F32)],
        compiler_params=_params(2),
        name="mlstm_prompt",
    )(qkvo, qkvo, qkvo, qkvo, gates, mh_g)


def _mlstm_sample_kernel(q_ref, k_ref, v_ref, o_ref, g_ref, mhg_ref, c_ref, n_ref, mrep_ref,
                         *rest, n_heads, scale, seq_len):
    hm_ref, c_out, n_out, m_out = rest[-4:]
    h = pl.program_id(1)
    R = q_ref.shape[0]
    nseq = R // seq_len

    g = g_ref[...]
    row = lax.broadcasted_iota(jnp.int32, g.shape, 0)
    bsum = _segment_cumsum(_log_sigmoid(g), row % seq_len, seq_len)
    logi_col = _pick_lane(g, h)
    b_col = _pick_lane(bsum, h + n_heads)
    m_col = _pick_lane(mrep_ref[...], h)

    r2 = lax.broadcasted_iota(jnp.int32, (R, R), 0)
    c2 = lax.broadcasted_iota(jnp.int32, (R, R), 1)
    eye = r2 == c2
    same = (r2 // seq_len) == (c2 // seq_len)
    b_row = _col_to_row(b_col, eye)
    logi_row = _col_to_row(logi_col, eye)
    dmat = jnp.where(same & (c2 <= r2), b_col - b_row + logi_row, -jnp.inf)
    inter = b_col + m_col
    m_t = jnp.maximum(inter, jnp.max(dmat, axis=1, keepdims=True))
    w_inter = jnp.exp(inter - m_t) * scale

    q = q_ref[...]
    k = k_ref[...]
    v = v_ref[...]
    q32 = q.astype(F32)
    qk = lax.dot_general(q, k, (((1,), (1,)), ((), ())), preferred_element_type=F32) * scale
    s = qk * jnp.exp(dmat - m_t)
    num_s = jnp.dot(s.astype(BF16), v, preferred_element_type=F32)

    seq_of_row = lax.broadcasted_iota(jnp.int32, (R, 1), 0) // seq_len
    num_c = jnp.zeros(num_s.shape, F32)
    qn = jnp.zeros((R, 1), F32)
    for j in range(nseq):
        mine = seq_of_row == j
        cq = lax.dot_general(q, c_ref[j, 0].astype(BF16), (((1,), (1,)), ((), ())),
                             preferred_element_type=F32)
        num_c = jnp.where(mine, cq, num_c)
        qn = jnp.where(mine, jnp.sum(q32 * n_ref[j, 0], axis=1, keepdims=True), qn)
    num = w_inter * num_c + num_s
    den = w_inter * qn + jnp.sum(s, axis=1, keepdims=True)
    hh = num * (1.0 / jnp.maximum(jnp.abs(den), jnp.exp(-m_t)))
    hm_ref[...] = _head_norm_gate(hh, o_ref[...], mhg_ref[...])

    pick_last = same & (c2 % seq_len == seq_len - 1)

    def last_of_seq(col):
        return jnp.sum(jnp.where(pick_last, _col_to_row(col, eye), 0.0), axis=1, keepdims=True)

    m_new = last_of_seq(m_t)
    w_c = jnp.exp(last_of_seq(inter) - m_new)
    w_s = jnp.exp(last_of_seq(b_col) - b_col + logi_col - m_new)
    vs = (w_s * v.astype(F32)).astype(BF16)
    kw = w_s * k.astype(F32)
    for j in range(nseq):
        mine = seq_of_row == j
        upd = lax.dot_general(jnp.where(mine, vs, jnp.zeros_like(vs)), k,
                              (((0,), (0,)), ((), ())), preferred_element_type=F32)
        wc_j = w_c[j * seq_len:j * seq_len + 1, :]
        c_out[j, 0] = wc_j * c_ref[j, 0] + upd
        n_out[j, 0] = wc_j * n_ref[j, 0] + jnp.sum(jnp.where(mine, kw, 0.0), axis=0, keepdims=True)
    m_out[0] = m_new


def _mlstm_sample(qkvo, gates, mh_g, state_c, state_n, m_rep, hm, c_prev, layer, row0,
                  n_seq, seq_len, n_heads, dqk, dv):
    nb = _tile(n_seq, SAMPLE_SEQS, 1)
    R = nb * seq_len
    assert row0 % R == 0
    blk0 = row0 // R
    qw, vw = n_heads * dqk, n_heads * dv
    o_blk0 = (2 * qw + vw) // dv
    ts = n_seq * seq_len
    any_spec = pl.BlockSpec(memory_space=pl.ANY)

    in_specs = [
        pl.BlockSpec((R, dqk), lambda i, h: (blk0 + i, h)),
        pl.BlockSpec((R, dqk), lambda i, h: (blk0 + i, n_heads + h)),
        pl.BlockSpec((R, dv), lambda i, h: (blk0 + i, 2 * qw // dv + h)),
        pl.BlockSpec((R, dv), lambda i, h: (blk0 + i, o_blk0 + h)),
        pl.BlockSpec((R, LANES), lambda i, h: (blk0 + i, 0)),
        pl.BlockSpec((None, 1, dv), lambda i, h: (layer, 0, h)),
        pl.BlockSpec((None, nb, 1, dv, dqk), lambda i, h: (layer, i, h, 0, 0)),
        pl.BlockSpec((None, nb, 1, 1, dqk), lambda i, h: (layer, i, h, 0, 0)),
        pl.BlockSpec((R, n_heads), lambda i, h: (i, 0)),
        any_spec,
    ]
    args = [qkvo, qkvo, qkvo, qkvo, gates, mh_g, state_c, state_n, m_rep, hm]
    aliases = {len(args) - 1: 0}
    if c_prev is not None:
        in_specs.append(any_spec)
        args.append(c_prev)
        aliases[len(args) - 1] = 1

    return pl.pallas_call(
        functools.partial(_mlstm_sample_kernel, n_heads=n_heads, scale=dqk ** -0.5,
                          seq_len=seq_len),
        grid=(n_seq // nb, n_heads),
        in_specs=in_specs,
        out_specs=[
            pl.BlockSpec((R, dv), lambda i, h: (blk0 + i, h)),
            pl.BlockSpec((None, nb, 1, dv, dqk), lambda i, h: (layer, i, h, 0, 0)),
            pl.BlockSpec((nb, 1, 1, dqk), lambda i, h: (i, h, 0, 0)),
            pl.BlockSpec((1, R, 1), lambda i, h: (h, i, 0)),
        ],
        out_shape=[
            jax.ShapeDtypeStruct(hm.shape, hm.dtype),
            jax.ShapeDtypeStruct(state_c.shape, F32),
            jax.ShapeDtypeStruct(state_n.shape[1:], F32),
            jax.ShapeDtypeStruct((n_heads, ts, 1), F32),
        ],
        input_output_aliases=aliases,
        compiler_params=_params(2),
        name="mlstm_sample",
    )(*args)


def _pool_conv_prompt_kernel(u_ref, uh_ref, cb_ref, cc_ref, cx_ref, cch_ref, cxh_ref,
                             maps_ref, scale_ref, cw_ref,
                             zp_ref, zs_ref, ps_ref, cs_ref, full_s, pf_s, *, windows):
    t = pl.program_id(1)
    last = pl.num_programs(1) - 1
    tr = u_ref.shape[0]
    gw = maps_ref.shape[1]
    n_buf = ps_ref.shape[1]
    taps = cw_ref.shape[0]
    has_past = t > 0

    full_s[0:POOL_HALO, :] = jnp.where(has_past, uh_ref[...], 0.0)
    full_s[POOL_HALO:, :] = u_ref[...]
    pf_s[0:CONV_HALO, :] = jnp.where(has_past, cch_ref[...] * cxh_ref[...], 0.0)
    pf_s[CONV_HALO:, :] = cc_ref[...] * cx_ref[...]
    pos = t * tr + lax.broadcasted_iota(jnp.int32, (tr, 1), 0)
    for g, w in enumerate(windows):
        cols = slice(g * gw, (g + 1) * gw)
        u = u_ref[:, cols]
        if w & (w - 1) == 0 and w <= POOL_HALO:
            win = full_s[:, cols]
            span = 1
            while span < w:
                win = win + pltpu.roll(win, span, axis=0)
                span *= 2
            acc = win[POOL_HALO:, :]
        else:
            acc = u
            for j in range(1, w):
                acc = acc + full_s[POOL_HALO - j:POOL_HALO - j + tr, cols]
        inv_cnt = 1.0 / jnp.minimum(pos + 1, w).astype(F32)
        zf = acc * inv_cnt - u
        zp = jnp.dot(zf.astype(BF16), maps_ref[g], preferred_element_type=F32) * scale_ref[:, cols]
        zp_ref[:, cols] = zp.astype(BF16)

        y = pf_s[CONV_HALO:CONV_HALO + tr, cols] * cw_ref[taps - 1:taps, cols]
        for j in range(taps - 1):
            back = taps - 1 - j
            y = y + pf_s[CONV_HALO - back:CONV_HALO - back + tr, cols] * cw_ref[j:j + 1, cols]
        zs_ref[:, cols] = (cb_ref[:, cols] * y).astype(BF16)

    @pl.when(t == last)
    def _():
        ps_ref[0] = full_s[POOL_HALO + tr - n_buf:POOL_HALO + tr, :]
        cs_ref[0] = pf_s[CONV_HALO + tr - (taps - 1):CONV_HALO + tr, :]


def _pool_conv_prompt(tail, maps, pscale, conv_w, layer, batch, seq, width, n_buf):
    tr = _tile(seq, 256, POOL_HALO)
    nt = seq // tr
    taps = conv_w.shape[1]
    n_groups, gw = maps.shape[1], maps.shape[2]
    hp = tr // POOL_HALO
    hc = tr // CONV_HALO

    def rows(b, t):
        return b * nt + t

    def main(col):
        return pl.BlockSpec((tr, width), lambda b, t: (rows(b, t), col))

    def halo(col, h, per_tile):
        return pl.BlockSpec((h, width), lambda b, t: (jnp.maximum(rows(b, t) * per_tile - 1, 0), col))

    return pl.pallas_call(
        functools.partial(_pool_conv_prompt_kernel, windows=POOL_WINDOWS),
        grid=(batch, nt),
        in_specs=[main(0), halo(0, POOL_HALO, hp), main(1), main(2), main(3),
                  halo(2, CONV_HALO, hc), halo(3, CONV_HALO, hc),
                  pl.BlockSpec((None, n_groups, gw, gw), lambda b, t: (layer, 0, 0, 0)),
                  pl.BlockSpec((None, 1, width), lambda b, t: (layer, 0, 0)),
                  pl.BlockSpec((None, taps, width), lambda b, t: (layer, 0, 0))],
        out_specs=[pl.BlockSpec((tr, width), lambda b, t: (rows(b, t), 0)),
                   pl.BlockSpec((tr, width), lambda b, t: (rows(b, t), 0)),
                   pl.BlockSpec((1, n_buf, width), lambda b, t: (b, 0, 0)),
                   pl.BlockSpec((1, taps - 1, width), lambda b, t: (b, 0, 0))],
        out_shape=[jax.ShapeDtypeStruct((tail.shape[0], width), BF16),
                   jax.ShapeDtypeStruct((tail.shape[0], width), BF16),
                   jax.ShapeDtypeStruct((batch, n_buf, width), F32),
                   jax.ShapeDtypeStruct((batch, taps - 1, width), F32)],
        scratch_shapes=[pltpu.VMEM((POOL_HALO + tr, width), F32),
                        pltpu.VMEM((CONV_HALO + tr, width), F32)],
        compiler_params=_params(2),
        name="pool_conv_prompt",
    )(tail, tail, tail, tail, tail, tail, tail, maps, pscale, conv_w)


def _pool_conv_sample_kernel(x_ref, ps_ref, cs_ref, maps_ref, scale_ref, cw_ref,
                             zp_ref, zs_ref, pn_ref, cn_ref, *, windows, start_pos):
    g = pl.program_id(0)
    seq_len = x_ref.shape[1]
    n_buf = ps_ref.shape[0]
    taps = cw_ref.shape[0]
    win = jnp.int32(windows[0])
    for i, w in enumerate(windows):
        win = jnp.where(g == i, jnp.int32(w), win)

    hist = [ps_ref[j] for j in range(n_buf)] + [x_ref[0, t] for t in range(seq_len)]
    for t in range(seq_len):
        cur = n_buf + t
        acc = hist[cur]
        for j in range(1, max(windows)):
            acc = acc + jnp.where(j < win, hist[cur - j], 0.0)
        cnt = jnp.minimum(start_pos + t + 1, win).astype(F32)
        zf = acc / cnt - hist[cur]
        zp = jnp.dot(zf.astype(BF16), maps_ref[0], preferred_element_type=F32) * scale_ref[...]
        zp_ref[t] = zp.astype(BF16)
    for j in range(n_buf):
        pn_ref[j] = hist[seq_len + j]

    prod = [cs_ref[j] for j in range(taps - 1)] + [x_ref[2, t] * x_ref[3, t] for t in range(seq_len)]
    for t in range(seq_len):
        y = prod[t] * cw_ref[0:1, :]
        for j in range(1, taps):
            y = y + prod[t + j] * cw_ref[j:j + 1, :]
        zs_ref[t] = (x_ref[1, t] * y).astype(BF16)
    for j in range(taps - 1):
        cn_ref[j] = prod[seq_len + j]


def _pool_conv_sample(x4, pstate_t, cstate_t, maps, pscale, conv_w, layer):
    _, seq_len, n_seq, width = x4.shape
    n_buf = pstate_t.shape[0]
    taps = conv_w.shape[1]
    n_groups, gw = maps.shape[1], maps.shape[2]
    assert len(POOL_WINDOWS) == n_groups and n_buf >= max(POOL_WINDOWS) - 1

    return pl.pallas_call(
        functools.partial(_pool_conv_sample_kernel, windows=POOL_WINDOWS, start_pos=PAST_LEN),
        grid=(n_groups,),
        in_specs=[pl.BlockSpec((4, seq_len, n_seq, gw), lambda g: (0, 0, 0, g)),
                  pl.BlockSpec((n_buf, n_seq, gw), lambda g: (0, 0, g)),
                  pl.BlockSpec((taps - 1, n_seq, gw), lambda g: (0, 0, g)),
                  pl.BlockSpec((None, 1, gw, gw), lambda g: (layer, g, 0, 0)),
                  pl.BlockSpec((None, 1, gw), lambda g: (layer, 0, g)),
                  pl.BlockSpec((None, taps, gw), lambda g: (layer, 0, g))],
        out_specs=[pl.BlockSpec((seq_len, n_seq, gw), lambda g: (0, 0, g)),
                   pl.BlockSpec((seq_len, n_seq, gw), lambda g: (0, 0, g)),
                   pl.BlockSpec((n_buf, n_seq, gw), lambda g: (0, 0, g)),
                   pl.BlockSpec((taps - 1, n_seq, gw), lambda g: (0, 0, g))],
        out_shape=[jax.ShapeDtypeStruct((seq_len, n_seq, width), BF16),
                   jax.ShapeDtypeStruct((seq_len, n_seq, width), BF16),
                   jax.ShapeDtypeStruct(pstate_t.shape, F32),
                   jax.ShapeDtypeStruct(cstate_t.shape, F32)],
        compiler_params=_params(1),
        name="pool_conv_sample",
    )(x4, pstate_t, cstate_t, maps, pscale, conv_w)


def kernel(x_prompt, x_sample, state_C, state_n, state_m, state_pool, state_conv, w_in, b_in, mh_g, pool_maps, pool_scale, conv_w, w_bm, w_bp, w_bs, w_o, ln1_g, ln1_b, w_gate, w_up, w_down, ln2_g, ln2_b):
    depth = w_in.shape[0]
    batch, seq, d = x_prompt.shape
    n_seq, seq_len, _ = x_sample.shape
    n_heads, dv, dqk = state_C.shape[2], state_C.shape[3], state_C.shape[4]
    n_buf, pw = state_pool.shape[2], state_pool.shape[3]
    cw = conv_w.shape[2]
    dff = w_gate.shape[2]
    assert pw == cw and dv == 2 * dqk and 2 * n_heads <= LANES
    alpha = (2 * depth) ** 0.25
    tp, ts = batch * seq, n_seq * seq_len
    t_all = tp + ts

    off_i = 2 * n_heads * dqk + 2 * n_heads * dv
    off_p = off_i + 2 * n_heads
    gate_col0 = pw + 3 * cw
    k_blocks = 2 if dff % (2 * LANES) == 0 else 1

    w_in_t = jnp.swapaxes(w_in, 1, 2)
    w_if = jnp.pad(w_in[:, :, off_i:off_p], ((0, 0), (0, 0), (0, LANES - (off_p - off_i)))).astype(BF16)
    b3 = b_in[:, None, :]
    b_qkvo, b_tail = b3[:, :, :off_i], b3[:, :, off_p:]
    b_if = jnp.pad(b3[:, :, off_i:off_p], ((0, 0), (0, 0), (0, LANES - (off_p - off_i))))
    wbm, wbp, wbs = (_to_bf16(w) for w in (w_bm, w_bp, w_bs))
    maps = pool_maps.astype(BF16)
    mhg3, pscale3 = mh_g[:, None, :], pool_scale[:, None, :]
    ln1g, ln1b, ln2g, ln2b = (a[:, None, :] for a in (ln1_g, ln1_b, ln2_g, ln2_b))
    state_n5 = state_n[:, :, :, None, :]
    pool_t = state_pool.transpose(0, 2, 1, 3)
    conv_t = state_conv.transpose(0, 2, 1, 3)

    x, xb = _stack_rows(x_prompt.reshape(tp, d), x_sample.reshape(ts, d))

    outs = [[] for _ in range(9)]
    c_s = None
    for l in range(depth):
        qkvo = _matmul_nt(xb, w_in_t, l, 0, off_i, b_qkvo, out_dtype=BF16, name="in_qkvo")
        gates = _matmul(xb, w_if, l, bias=b_if, name="in_if")
        tail = _matmul_nt(xb, w_in_t, l, off_p, w_in.shape[2] - off_p, b_tail, name="in_tail")

        hm, c_p, n_p, m_p = _mlstm_prompt(qkvo, gates, mhg3, l, batch, seq, t_all, n_heads, dqk, dv)
        m_rep = jnp.repeat(state_m[l], seq_len, axis=0)
        hm, c_s, n_s, m_s = _mlstm_sample(qkvo, gates, mhg3, state_C, state_n5, m_rep, hm, c_s, l,
                                          tp, n_seq, seq_len, n_heads, dqk, dv)

        zp, zs, ps_p, cs_p = _pool_conv_prompt(tail, maps, pscale3, conv_w, l, batch, seq, pw, n_buf)
        x4 = tail[tp:, :pw + 3 * cw].reshape(n_seq, seq_len, 4, pw).transpose(2, 1, 0, 3)
        zp_s, zs_s, ps_s, cs_s = _pool_conv_sample(x4, pool_t[l], conv_t[l], maps, pscale3, conv_w, l)
        zp = lax.dynamic_update_slice(zp, zp_s.transpose(1, 0, 2).reshape(ts, pw), (tp, 0))
        zs = lax.dynamic_update_slice(zs, zs_s.transpose(1, 0, 2).reshape(ts, cw), (tp, 0))

        mixed = _merge(hm, zp, zs, wbm, wbp, wbs, l, tail, gate_col0)
        r1 = _matmul(mixed, w_o, l, res=x, res_scale=alpha, name="out_proj")
        x1, x1b = _layernorm(r1, ln1g, ln1b, l)

        ff = _ffn_up(x1b, w_gate, w_up, l, tn=FF_TILE)
        r2 = _matmul(ff, w_down, l, res=x1, res_scale=alpha, tn=FF_TILE, k_blocks=k_blocks,
                     name="ffn_down")
        for kb in range(1, k_blocks):
            r2 = _matmul(ff, w_down, l, res=r2, tn=FF_TILE, k_blocks=k_blocks, k_index=kb,
                         name="ffn_down")
        if l + 1 < depth:
            x, xb = _layernorm(r2, ln2g, ln2b, l)
        else:
            y_p, = _layernorm(r2, ln2g, ln2b, l, 0, tp, with_bf16=False)
            y_s, = _layernorm(r2, ln2g, ln2b, l, tp, ts, with_bf16=False)

        new = (c_p, n_p, m_p[:, :, 0], ps_p, cs_p,
               n_s[:, :, 0, :], m_s[:, seq_len - 1::seq_len, 0].T,
               ps_s.transpose(1, 0, 2), cs_s.transpose(1, 0, 2))
        for acc, val in zip(outs, new):
            acc.append(val)

    st = [jnp.stack(o) for o in outs]
    return (y_p.reshape(batch, seq, d), y_s.reshape(n_seq, seq_len, d),
            st[0], st[1], st[2], st[3], st[4], c_s, st[5], st[6], st[7], st[8])
```

```python
import functools

import jax
import jax.numpy as jnp
from jax import lax
from jax.experimental import pallas as pl
from jax.experimental.pallas import tpu as pltpu

F32 = jnp.float32
BF16 = jnp.bfloat16

POOL_WINDOWS = (2, 4, 8, 16)
PAST_LEN = 16384
LN_EPS = 1e-5
PROMPT_CHUNK = 128

LANES = 128
BF16_SUBLANES = 16
VMEM_LIMIT = 56 * 1024 * 1024
POOL_HALO = 16
CONV_HALO = 8
HEAD_GROUP = 2
SAMPLE_SEQS = 16
FF_TILE = 256


def _params(n_axes):
    return pltpu.CompilerParams(dimension_semantics=("arbitrary",) * n_axes,
                                vmem_limit_bytes=VMEM_LIMIT)


def _tile(dim, target, mult):
    best = None
    for d in range(mult, min(dim, target) + 1, mult):
        if dim % d == 0:
            best = d
    return dim if best is None else best


def _log_sigmoid(x):
    return jnp.minimum(x, 0.0) - jnp.log(1.0 + jnp.exp(-jnp.abs(x)))


def _cast_kernel(x_ref, o_ref):
    o_ref[...] = x_ref[...].astype(o_ref.dtype)


def _to_bf16(w):
    layers, rows, cols = w.shape
    tc = _tile(cols, 4096, LANES)
    tr = _tile(rows, max(BF16_SUBLANES, 1024 * 1024 // tc), BF16_SUBLANES)
    spec = pl.BlockSpec((None, tr, tc), lambda l, i, j: (l, i, j))
    return pl.pallas_call(
        _cast_kernel,
        grid=(layers, rows // tr, cols // tc),
        in_specs=[spec],
        out_specs=spec,
        out_shape=jax.ShapeDtypeStruct((layers, rows, cols), BF16),
        compiler_params=_params(3),
        name="to_bf16",
    )(w)


def _stack_kernel(xp_ref, xs_ref, of_ref, ob_ref, *, prompt_tiles):
    x = jnp.where(pl.program_id(0) < prompt_tiles, xp_ref[...], xs_ref[...])
    of_ref[...] = x
    ob_ref[...] = x.astype(BF16)


def _stack_rows(xp, xs):
    tp, d = xp.shape
    ts = xs.shape[0]
    tr = BF16_SUBLANES
    for cand in range(BF16_SUBLANES, 256 + 1, BF16_SUBLANES):
        if tp % cand == 0 and ts % cand == 0:
            tr = cand
    n_p, n_s = tp // tr, ts // tr
    out = pl.BlockSpec((tr, d), lambda i: (i, 0))
    return pl.pallas_call(
        functools.partial(_stack_kernel, prompt_tiles=n_p),
        grid=(n_p + n_s,),
        in_specs=[pl.BlockSpec((tr, d), lambda i: (jnp.minimum(i, n_p - 1), 0)),
                  pl.BlockSpec((tr, d), lambda i: (jnp.maximum(i - n_p, 0), 0))],
        out_specs=[out, out],
        out_shape=[jax.ShapeDtypeStruct((tp + ts, d), F32),
                   jax.ShapeDtypeStruct((tp + ts, d), BF16)],
        compiler_params=_params(1),
        name="stack_rows",
    )(xp, xs)


def _mm_kernel(*refs, has_bias, has_res, res_scale):
    x_ref, w_ref = refs[0], refs[1]
    o_ref = refs[-1]
    acc = jnp.dot(x_ref[...], w_ref[...].astype(BF16), preferred_element_type=F32)
    pos = 2
    if has_bias:
        acc = acc + refs[pos][...]
        pos += 1
    if has_res:
        acc = acc + res_scale * refs[pos][...]
    o_ref[...] = acc.astype(o_ref.dtype)


def _matmul(x, w, layer, bias=None, res=None, res_scale=1.0, out_dtype=F32, tm=1088, tn=512,
            k_blocks=1, k_index=0, name="matmul"):
    t = x.shape[0]
    n = w.shape[2]
    k = x.shape[1] // k_blocks
    tm = _tile(t, tm, BF16_SUBLANES)
    tn = _tile(n, tn, LANES)
    in_specs = [pl.BlockSpec((tm, k), lambda i, j: (i, k_index)),
                pl.BlockSpec((None, k, tn), lambda i, j: (layer, k_index, j))]
    args = [x, w]
    if bias is not None:
        in_specs.append(pl.BlockSpec((None, 1, tn), lambda i, j: (layer, 0, j)))
        args.append(bias)
    if res is not None:
        in_specs.append(pl.BlockSpec((tm, tn), lambda i, j: (i, j)))
        args.append(res)
    return pl.pallas_call(
        functools.partial(_mm_kernel, has_bias=bias is not None, has_res=res is not None,
                          res_scale=res_scale),
        grid=(t // tm, n // tn),
        in_specs=in_specs,
        out_specs=pl.BlockSpec((tm, tn), lambda i, j: (i, j)),
        out_shape=jax.ShapeDtypeStruct((t, n), out_dtype),
        compiler_params=_params(2),
        name=name,
    )(*args)


def _mm_nt_kernel(x_ref, a_ref, *rest, shift):
    o_ref, b_ref = rest[-1], rest[-2]
    w = a_ref[...].astype(BF16)
    if shift:
        w = jnp.concatenate([w[shift:], rest[0][...].astype(BF16)], axis=0)
    acc = lax.dot_general(x_ref[...], w, (((1,), (1,)), ((), ())), preferred_element_type=F32)
    o_ref[...] = (acc + b_ref[...]).astype(o_ref.dtype)


def _matmul_nt(x, wt, layer, row0, n, bias, out_dtype=F32, tm=2176, tn=512, name="matmul_nt"):
    t, k = x.shape
    tm = _tile(t, tm, BF16_SUBLANES)
    tn = _tile(n, tn, LANES)
    shift = row0 % tn
    base = row0 - shift
    assert shift % BF16_SUBLANES == 0 and (shift == 0 or tn % shift == 0)
    in_specs = [pl.BlockSpec((tm, k), lambda i, j: (i, 0), pipeline_mode=pl.Buffered(1)),
                pl.BlockSpec((None, tn, k), lambda i, j: (layer, base // tn + j, 0))]
    args = [x, wt]
    if shift:
        in_specs.append(pl.BlockSpec((None, shift, k),
                                     lambda i, j: (layer, (base + (j + 1) * tn) // shift, 0)))
        args.append(wt)
    in_specs.append(pl.BlockSpec((None, 1, tn), lambda i, j: (layer, 0, j)))
    args.append(bias)
    return pl.pallas_call(
        functools.partial(_mm_nt_kernel, shift=shift),
        grid=(t // tm, n // tn),
        in_specs=in_specs,
        out_specs=pl.BlockSpec((tm, tn), lambda i, j: (i, j)),
        out_shape=jax.ShapeDtypeStruct((t, n), out_dtype),
        compiler_params=_params(2),
        name=name,
    )(*args)


def _ln_kernel(r_ref, g_ref, b_ref, of_ref, *maybe_bf16_ref):
    r = r_ref[...]
    mu = jnp.mean(r, axis=1, keepdims=True)
    d = r - mu
    var = jnp.mean(d * d, axis=1, keepdims=True)
    y = d * lax.rsqrt(var + LN_EPS) * g_ref[...] + b_ref[...]
    of_ref[...] = y
    for ob_ref in maybe_bf16_ref:
        ob_ref[...] = y.astype(BF16)


def _layernorm(r, g, b, layer, row0=0, n_rows=None, with_bf16=True):
    d = r.shape[1]
    n_rows = r.shape[0] - row0 if n_rows is None else n_rows
    tr = BF16_SUBLANES
    for cand in range(BF16_SUBLANES, 544 + 1, BF16_SUBLANES):
        if row0 % cand == 0 and n_rows % cand == 0:
            tr = cand
    blk0 = row0 // tr
    vec = pl.BlockSpec((None, 1, d), lambda i: (layer, 0, 0))
    out_row = pl.BlockSpec((tr, d), lambda i: (i, 0))
    out_specs = [out_row]
    out_shape = [jax.ShapeDtypeStruct((n_rows, d), F32)]
    if with_bf16:
        out_specs.append(out_row)
        out_shape.append(jax.ShapeDtypeStruct((n_rows, d), BF16))
    return pl.pallas_call(
        _ln_kernel,
        grid=(n_rows // tr,),
        in_specs=[pl.BlockSpec((tr, d), lambda i: (blk0 + i, 0)), vec, vec],
        out_specs=out_specs,
        out_shape=out_shape,
        compiler_params=_params(1),
        name="layernorm",
    )(r, g, b)


def _ffn_up_kernel(x_ref, wg_ref, wu_ref, o_ref):
    x = x_ref[...]
    a = jnp.dot(x, wg_ref[...].astype(BF16), preferred_element_type=F32)
    u = jnp.dot(x, wu_ref[...].astype(BF16), preferred_element_type=F32)
    o_ref[...] = (a * jax.nn.sigmoid(a) * u).astype(o_ref.dtype)


def _ffn_up(x, wg, wu, layer, tm=2176, tn=512):
    t, k = x.shape
    n = wg.shape[2]
    tm = _tile(t, tm, BF16_SUBLANES)
    tn = _tile(n, tn, LANES)
    wspec = pl.BlockSpec((None, k, tn), lambda i, j: (layer, 0, j))
    return pl.pallas_call(
        _ffn_up_kernel,
        grid=(t // tm, n // tn),
        in_specs=[pl.BlockSpec((tm, k), lambda i, j: (i, 0), pipeline_mode=pl.Buffered(1)),
                  wspec, wspec],
        out_specs=pl.BlockSpec((tm, tn), lambda i, j: (i, j)),
        out_shape=jax.ShapeDtypeStruct((t, n), BF16),
        compiler_params=_params(2),
        name="ffn_up",
    )(x, wg, wu)


def _merge_kernel(hm_ref, zp_ref, zs_ref, wm_ref, wp_ref, ws_ref, gm_ref, gp_ref, gs_ref, o_ref):
    ym = jnp.dot(hm_ref[...], wm_ref[...], preferred_element_type=F32)
    yp = jnp.dot(zp_ref[...], wp_ref[...], preferred_element_type=F32)
    ys = jnp.dot(zs_ref[...], ws_ref[...], preferred_element_type=F32)
    mixed = (jax.nn.sigmoid(gm_ref[...]) * ym + jax.nn.sigmoid(gp_ref[...]) * yp
             + jax.nn.sigmoid(gs_ref[...]) * ys)
    o_ref[...] = mixed.astype(o_ref.dtype)


def _merge(hm, zp, zs, wm, wp, ws, layer, tail, gate_col0, tm=544, tn=512):
    t = hm.shape[0]
    d = wm.shape[2]
    tm = _tile(t, tm, BF16_SUBLANES)
    tn = _tile(d, tn, LANES)
    g0 = gate_col0 // tn
    gd = d // tn

    def xspec(a):
        return pl.BlockSpec((tm, a.shape[1]), lambda i, j: (i, 0))

    def wspec(a):
        return pl.BlockSpec((None, a.shape[1], tn), lambda i, j: (layer, 0, j))

    def gspec(branch):
        return pl.BlockSpec((tm, tn), lambda i, j: (i, g0 + branch * gd + j))

    return pl.pallas_call(
        _merge_kernel,
        grid=(t // tm, d // tn),
        in_specs=[xspec(hm), xspec(zp), xspec(zs), wspec(wm), wspec(wp), wspec(ws),
                  gspec(0), gspec(1), gspec(2)],
        out_specs=pl.BlockSpec((tm, tn), lambda i, j: (i, j)),
        out_shape=jax.ShapeDtypeStruct((t, d), BF16),
        compiler_params=_params(2),
        name="merge",
    )(hm, zp, zs, wm, wp, ws, tail, tail, tail)


def _pick_lane(x, lane_index):
    lane = lax.broadcasted_iota(jnp.int32, x.shape, 1)
    return jnp.sum(jnp.where(lane == lane_index, x, 0.0), axis=1, keepdims=True)


def _col_to_row(col, eye):
    return jnp.sum(jnp.where(eye, col, 0.0), axis=0, keepdims=True)


def _segment_cumsum(x, seg_pos, seg_len):
    s = 1
    while s < seg_len:
        x = x + jnp.where(seg_pos >= s, pltpu.roll(x, s, axis=0), 0.0)
        s *= 2
    return x


def _head_norm_gate(hh, o_pre, mh_g):
    mu = jnp.mean(hh, axis=1, keepdims=True)
    d = hh - mu
    var = jnp.mean(d * d, axis=1, keepdims=True)
    return (jax.nn.sigmoid(o_pre.astype(F32)) * (d * lax.rsqrt(var + LN_EPS) * mh_g)).astype(BF16)


def _mlstm_prompt_kernel(q_ref, k_ref, v_ref, o_ref, g_ref, mhg_ref,
                         hm_ref, c_out, n_out, m_out,
                         ct_s, n_s, m_s, *, n_heads, dqk, dv, scale):
    c = pl.program_id(1)
    last = pl.num_programs(1) - 1
    L = q_ref.shape[0]

    @pl.when(c == 0)
    def _():
        ct_s[...] = jnp.zeros_like(ct_s)
        n_s[...] = jnp.zeros_like(n_s)
        m_s[...] = jnp.zeros_like(m_s)

    g = g_ref[...]
    row = lax.broadcasted_iota(jnp.int32, g.shape, 0)
    lane = lax.broadcasted_iota(jnp.int32, g.shape, 1)
    bsum = _segment_cumsum(_log_sigmoid(g), row, L)
    gb = jnp.where(lane < n_heads, g, bsum)
    gb_t = gb.T
    r2 = lax.broadcasted_iota(jnp.int32, (L, L), 0)
    c2 = lax.broadcasted_iota(jnp.int32, (L, L), 1)
    causal = r2 >= c2

    for h0 in range(0, n_heads, HEAD_GROUP):
        heads = range(h0, min(h0 + HEAD_GROUP, n_heads))
        st = {}
        for h in heads:
            logi_col = gb[:, h:h + 1]
            b_col = gb[:, n_heads + h:n_heads + h + 1]
            logi_row = gb_t[h:h + 1, :]
            b_row = gb_t[n_heads + h:n_heads + h + 1, :]
            dmat = jnp.where(causal, b_col - b_row + logi_row, -jnp.inf)
            inter = b_col + m_s[h:h + 1, :]
            m_t = jnp.maximum(inter, jnp.max(dmat, axis=1, keepdims=True))
            st[h] = dict(logi_col=logi_col, b_col=b_col, inter=inter, m_t=m_t,
                         w_inter=jnp.exp(inter - m_t) * scale,
                         decay=jnp.exp(dmat - m_t))
        for h in heads:
            q = q_ref[:, h * dqk:(h + 1) * dqk]
            k = k_ref[:, h * dqk:(h + 1) * dqk]
            ct = ct_s[h]
            qk = lax.dot_general(q, k, (((1,), (1,)), ((), ())), preferred_element_type=F32)
            st[h]["s"] = qk * scale * st[h]["decay"]
            st[h]["qc"] = jnp.dot(q, ct.astype(BF16), preferred_element_type=F32)
            st[h]["qn"] = jnp.sum(q.astype(F32) * n_s[h:h + 1, :], axis=1, keepdims=True)
        for h in heads:
            e = st[h]
            v = v_ref[:, h * dv:(h + 1) * dv]
            num = e["w_inter"] * e["qc"] + jnp.dot(e["s"].astype(BF16), v,
                                                   preferred_element_type=F32)
            den = e["w_inter"] * e["qn"] + jnp.sum(e["s"], axis=1, keepdims=True)
            hh = num * (1.0 / jnp.maximum(jnp.abs(den), jnp.exp(-e["m_t"])))
            hm_ref[:, h * dv:(h + 1) * dv] = _head_norm_gate(
                hh, o_ref[:, h * dv:(h + 1) * dv], mhg_ref[:, h * dv:(h + 1) * dv])
        for h in heads:
            e = st[h]
            k = k_ref[:, h * dqk:(h + 1) * dqk]
            v = v_ref[:, h * dv:(h + 1) * dv]
            m_new = e["m_t"][L - 1:L, :]
            w_c = jnp.exp(e["inter"][L - 1:L, :] - m_new)
            w_s = jnp.exp(e["b_col"][L - 1:L, :] - e["b_col"] + e["logi_col"] - m_new)
            vs = (w_s * v.astype(F32)).astype(BF16)
            ct_s[h] = w_c * ct_s[h] + lax.dot_general(k, vs, (((0,), (0,)), ((), ())),
                                                      preferred_element_type=F32)
            n_s[h:h + 1, :] = (w_c * n_s[h:h + 1, :]
                               + jnp.sum(w_s * k.astype(F32), axis=0, keepdims=True))
            m_s[h:h + 1, :] = m_new

    @pl.when(c == last)
    def _():
        for h in range(n_heads):
            c_out[0, h] = ct_s[h].T
        n_out[0] = n_s[...]
        m_out[0] = m_s[...]


def _mlstm_prompt(qkvo, gates, mh_g, layer, batch, seq, n_rows, n_heads, dqk, dv):
    L = PROMPT_CHUNK if seq % PROMPT_CHUNK == 0 else seq
    nc = seq // L
    qw, vw = n_heads * dqk, n_heads * dv
    assert qw % LANES == 0 and 2 * qw % vw == 0

    def rows(b, c):
        return b * nc + c

    return pl.pallas_call(
        functools.partial(_mlstm_prompt_kernel, n_heads=n_heads, dqk=dqk, dv=dv,
                          scale=dqk ** -0.5),
        grid=(batch, nc),
        in_specs=[
            pl.BlockSpec((L, qw), lambda b, c: (rows(b, c), 0)),
            pl.BlockSpec((L, qw), lambda b, c: (rows(b, c), 1)),
            pl.BlockSpec((L, vw), lambda b, c: (rows(b, c), 2 * qw // vw)),
            pl.BlockSpec((L, vw), lambda b, c: (rows(b, c), 2 * qw // vw + 1)),
            pl.BlockSpec((L, LANES), lambda b, c: (rows(b, c), 0)),
            pl.BlockSpec((None, 1, vw), lambda b, c: (layer, 0, 0)),
        ],
        out_specs=[
            pl.BlockSpec((L, vw), lambda b, c: (rows(b, c), 0)),
            pl.BlockSpec((1, n_heads, dv, dqk), lambda b, c: (b, 0, 0, 0)),
            pl.BlockSpec((1, n_heads, dqk), lambda b, c: (b, 0, 0)),
            pl.BlockSpec((1, n_heads, 1), lambda b, c: (b, 0, 0)),
        ],
        out_shape=[
            jax.ShapeDtypeStruct((n_rows, vw), BF16),
            jax.ShapeDtypeStruct((batch, n_heads, dv, dqk), F32),
            jax.ShapeDtypeStruct((batch, n_heads, dqk), F32),
            jax.ShapeDtypeStruct((batch, n_heads, 1), F32),
        ],
        scratch_shapes=[pltpu.VMEM((n_heads, dqk, dv), F32), pltpu.VMEM((n_heads, dqk), F32),
                        pltpu.VMEM((n_heads, 1), F32)],
        compiler_params=_params(2),
        name="mlstm_prompt",
    )(qkvo, qkvo, qkvo, qkvo, gates, mh_g)


def _mlstm_sample_kernel(q_ref, k_ref, v_ref, o_ref, g_ref, mhg_ref, c_ref, n_ref, mrep_ref,
                         *rest, n_heads, scale, seq_len):
    hm_ref, c_out, n_out, m_out = rest[-4:]
    h = pl.program_id(1)
    R = q_ref.shape[0]
    nseq = R // seq_len

    g = g_ref[...]
    row = lax.broadcasted_iota(jnp.int32, g.shape, 0)
    bsum = _segment_cumsum(_log_sigmoid(g), row % seq_len, seq_len)
    logi_col = _pick_lane(g, h)
    b_col = _pick_lane(bsum, h + n_heads)
    m_col = _pick_lane(mrep_ref[...], h)

    r2 = lax.broadcasted_iota(jnp.int32, (R, R), 0)
    c2 = lax.broadcasted_iota(jnp.int32, (R, R), 1)
    eye = r2 == c2
    same = (r2 // seq_len) == (c2 // seq_len)
    b_row = _col_to_row(b_col, eye)
    logi_row = _col_to_row(logi_col, eye)
    dmat = jnp.where(same & (c2 <= r2), b_col - b_row + logi_row, -jnp.inf)
    inter = b_col + m_col
    m_t = jnp.maximum(inter, jnp.max(dmat, axis=1, keepdims=True))
    w_inter = jnp.exp(inter - m_t) * scale

    q = q_ref[...]
    k = k_ref[...]
    v = v_ref[...]
    q32 = q.astype(F32)
    qk = lax.dot_general(q, k, (((1,), (1,)), ((), ())), preferred_element_type=F32) * scale
    s = qk * jnp.exp(dmat - m_t)
    num_s = jnp.dot(s.astype(BF16), v, preferred_element_type=F32)

    seq_of_row = lax.broadcasted_iota(jnp.int32, (R, 1), 0) // seq_len
    num_c = jnp.zeros(num_s.shape, F32)
    qn = jnp.zeros((R, 1), F32)
    for j in range(nseq):
        mine = seq_of_row == j
        cq = lax.dot_general(q, c_ref[j, 0].astype(BF16), (((1,), (1,)), ((), ())),
                             preferred_element_type=F32)
        num_c = jnp.where(mine, cq, num_c)
        qn = jnp.where(mine, jnp.sum(q32 * n_ref[j, 0], axis=1, keepdims=True), qn)
    num = w_inter * num_c + num_s
    den = w_inter * qn + jnp.sum(s, axis=1, keepdims=True)
    hh = num * (1.0 / jnp.maximum(jnp.abs(den), jnp.exp(-m_t)))
    hm_ref[...] = _head_norm_gate(hh, o_ref[...], mhg_ref[...])

    pick_last = same & (c2 % seq_len == seq_len - 1)

    def last_of_seq(col):
        return jnp.sum(jnp.where(pick_last, _col_to_row(col, eye), 0.0), axis=1, keepdims=True)

    m_new = last_of_seq(m_t)
    w_c = jnp.exp(last_of_seq(inter) - m_new)
    w_s = jnp.exp(last_of_seq(b_col) - b_col + logi_col - m_new)
    vs = (w_s * v.astype(F32)).astype(BF16)
    kw = w_s * k.astype(F32)
    for j in range(nseq):
        mine = seq_of_row == j
        upd = lax.dot_general(jnp.where(mine, vs, jnp.zeros_like(vs)), k,
                              (((0,), (0,)), ((), ())), preferred_element_type=F32)
        wc_j = w_c[j * seq_len:j * seq_len + 1, :]
        c_out[j, 0] = wc_j * c_ref[j, 0] + upd
        n_out[j, 0] = wc_j * n_ref[j, 0] + jnp.sum(jnp.where(mine, kw, 0.0), axis=0, keepdims=True)
    m_out[0] = m_new


def _mlstm_sample(qkvo, gates, mh_g, state_c, state_n, m_rep, hm, c_prev, layer, row0,
                  n_seq, seq_len, n_heads, dqk, dv):
    nb = _tile(n_seq, SAMPLE_SEQS, 1)
    R = nb * seq_len
    assert row0 % R == 0
    blk0 = row0 // R
    qw, vw = n_heads * dqk, n_heads * dv
    o_blk0 = (2 * qw + vw) // dv
    ts = n_seq * seq_len
    any_spec = pl.BlockSpec(memory_space=pl.ANY)

    in_specs = [
        pl.BlockSpec((R, dqk), lambda i, h: (blk0 + i, h)),
        pl.BlockSpec((R, dqk), lambda i, h: (blk0 + i, n_heads + h)),
        pl.BlockSpec((R, dv), lambda i, h: (blk0 + i, 2 * qw // dv + h)),
        pl.BlockSpec((R, dv), lambda i, h: (blk0 + i, o_blk0 + h)),
        pl.BlockSpec((R, LANES), lambda i, h: (blk0 + i, 0)),
        pl.BlockSpec((None, 1, dv), lambda i, h: (layer, 0, h)),
        pl.BlockSpec((None, nb, 1, dv, dqk), lambda i, h: (layer, i, h, 0, 0)),
        pl.BlockSpec((None, nb, 1, 1, dqk), lambda i, h: (layer, i, h, 0, 0)),
        pl.BlockSpec((R, n_heads), lambda i, h: (i, 0)),
        any_spec,
    ]
    args = [qkvo, qkvo, qkvo, qkvo, gates, mh_g, state_c, state_n, m_rep, hm]
    aliases = {len(args) - 1: 0}
    if c_prev is not None:
        in_specs.append(any_spec)
        args.append(c_prev)
        aliases[len(args) - 1] = 1

    return pl.pallas_call(
        functools.partial(_mlstm_sample_kernel, n_heads=n_heads, scale=dqk ** -0.5,
                          seq_len=seq_len),
        grid=(n_seq // nb, n_heads),
        in_specs=in_specs,
        out_specs=[
            pl.BlockSpec((R, dv), lambda i, h: (blk0 + i, h)),
            pl.BlockSpec((None, nb, 1, dv, dqk), lambda i, h: (layer, i, h, 0, 0)),
            pl.BlockSpec((nb, 1, 1, dqk), lambda i, h: (i, h, 0, 0)),
            pl.BlockSpec((1, R, 1), lambda i, h: (h, i, 0)),
        ],
        out_shape=[
            jax.ShapeDtypeStruct(hm.shape, hm.dtype),
            jax.ShapeDtypeStruct(state_c.shape, F32),
            jax.ShapeDtypeStruct(state_n.shape[1:], F32),
            jax.ShapeDtypeStruct((n_heads, ts, 1), F32),
        ],
        input_output_aliases=aliases,
        compiler_params=_params(2),
        name="mlstm_sample",
    )(*args)


def _pool_conv_prompt_kernel(u_ref, uh_ref, cb_ref, cc_ref, cx_ref, cch_ref, cxh_ref,
                             maps_ref, scale_ref, cw_ref,
                             zp_ref, zs_ref, ps_ref, cs_ref, full_s, pf_s, *, windows):
    t = pl.program_id(1)
    last = pl.num_programs(1) - 1
    tr = u_ref.shape[0]
    gw = maps_ref.shape[1]
    n_buf = ps_ref.shape[1]
    taps = cw_ref.shape[0]
    has_past = t > 0

    full_s[0:POOL_HALO, :] = jnp.where(has_past, uh_ref[...], 0.0)
    full_s[POOL_HALO:, :] = u_ref[...]
    pf_s[0:CONV_HALO, :] = jnp.where(has_past, cch_ref[...] * cxh_ref[...], 0.0)
    pf_s[CONV_HALO:, :] = cc_ref[...] * cx_ref[...]
    pos = t * tr + lax.broadcasted_iota(jnp.int32, (tr, 1), 0)
    for g, w in enumerate(windows):
        cols = slice(g * gw, (g + 1) * gw)
        u = u_ref[:, cols]
        if w & (w - 1) == 0 and w <= POOL_HALO:
            win = full_s[:, cols]
            span = 1
            while span < w:
                win = win + pltpu.roll(win, span, axis=0)
                span *= 2
            acc = win[POOL_HALO:, :]
        else:
            acc = u
            for j in range(1, w):
                acc = acc + full_s[POOL_HALO - j:POOL_HALO - j + tr, cols]
        inv_cnt = 1.0 / jnp.minimum(pos + 1, w).astype(F32)
        zf = acc * inv_cnt - u
        zp = jnp.dot(zf.astype(BF16), maps_ref[g], preferred_element_type=F32) * scale_ref[:, cols]
        zp_ref[:, cols] = zp.astype(BF16)

        y = pf_s[CONV_HALO:CONV_HALO + tr, cols] * cw_ref[taps - 1:taps, cols]
        for j in range(taps - 1):
            back = taps - 1 - j
            y = y + pf_s[CONV_HALO - back:CONV_HALO - back + tr, cols] * cw_ref[j:j + 1, cols]
        zs_ref[:, cols] = (cb_ref[:, cols] * y).astype(BF16)

    @pl.when(t == last)
    def _():
        ps_ref[0] = full_s[POOL_HALO + tr - n_buf:POOL_HALO + tr, :]
        cs_ref[0] = pf_s[CONV_HALO + tr - (taps - 1):CONV_HALO + tr, :]


def _pool_conv_prompt(tail, maps, pscale, conv_w, layer, batch, seq, width, n_buf):
    tr = _tile(seq, 256, POOL_HALO)
    nt = seq // tr
    taps = conv_w.shape[1]
    n_groups, gw = maps.shape[1], maps.shape[2]
    hp = tr // POOL_HALO
    hc = tr // CONV_HALO

    def rows(b, t):
        return b * nt + t

    def main(col):
        return pl.BlockSpec((tr, width), lambda b, t: (rows(b, t), col))

    def halo(col, h, per_tile):
        return pl.BlockSpec((h, width), lambda b, t: (jnp.maximum(rows(b, t) * per_tile - 1, 0), col))

    return pl.pallas_call(
        functools.partial(_pool_conv_prompt_kernel, windows=POOL_WINDOWS),
        grid=(batch, nt),
        in_specs=[main(0), halo(0, POOL_HALO, hp), main(1), main(2), main(3),
                  halo(2, CONV_HALO, hc), halo(3, CONV_HALO, hc),
                  pl.BlockSpec((None, n_groups, gw, gw), lambda b, t: (layer, 0, 0, 0)),
                  pl.BlockSpec((None, 1, width), lambda b, t: (layer, 0, 0)),
                  pl.BlockSpec((None, taps, width), lambda b, t: (layer, 0, 0))],
        out_specs=[pl.BlockSpec((tr, width), lambda b, t: (rows(b, t), 0)),
                   pl.BlockSpec((tr, width), lambda b, t: (rows(b, t), 0)),
                   pl.BlockSpec((1, n_buf, width), lambda b, t: (b, 0, 0)),
                   pl.BlockSpec((1, taps - 1, width), lambda b, t: (b, 0, 0))],
        out_shape=[jax.ShapeDtypeStruct((tail.shape[0], width), BF16),
                   jax.ShapeDtypeStruct((tail.shape[0], width), BF16),
                   jax.ShapeDtypeStruct((batch, n_buf, width), F32),
                   jax.ShapeDtypeStruct((batch, taps - 1, width), F32)],
        scratch_shapes=[pltpu.VMEM((POOL_HALO + tr, width), F32),
                        pltpu.VMEM((CONV_HALO + tr, width), F32)],
        compiler_params=_params(2),
        name="pool_conv_prompt",
    )(tail, tail, tail, tail, tail, tail, tail, maps, pscale, conv_w)


def _pool_conv_sample_kernel(x_ref, ps_ref, cs_ref, maps_ref, scale_ref, cw_ref,
                             zp_ref, zs_ref, pn_ref, cn_ref, *, windows, start_pos):
    g = pl.program_id(0)
    seq_len = x_ref.shape[1]
    n_buf = ps_ref.shape[0]
    taps = cw_ref.shape[0]
    win = jnp.int32(windows[0])
    for i, w in enumerate(windows):
        win = jnp.where(g == i, jnp.int32(w), win)

    hist = [ps_ref[j] for j in range(n_buf)] + [x_ref[0, t] for t in range(seq_len)]
    for t in range(seq_len):
        cur = n_buf + t
        acc = hist[cur]
        for j in range(1, max(windows)):
            acc = acc + jnp.where(j < win, hist[cur - j], 0.0)
        cnt = jnp.minimum(start_pos + t + 1, win).astype(F32)
        zf = acc / cnt - hist[cur]
        zp = jnp.dot(zf.astype(BF16), maps_ref[0], preferred_element_type=F32) * scale_ref[...]
        zp_ref[t] = zp.astype(BF16)
    for j in range(n_buf):
        pn_ref[j] = hist[seq_len + j]

    prod = [cs_ref[j] for j in range(taps - 1)] + [x_ref[2, t] * x_ref[3, t] for t in range(seq_len)]
    for t in range(seq_len):
        y = prod[t] * cw_ref[0:1, :]
        for j in range(1, taps):
            y = y + prod[t + j] * cw_ref[j:j + 1, :]
        zs_ref[t] = (x_ref[1, t] * y).astype(BF16)
    for j in range(taps - 1):
        cn_ref[j] = prod[seq_len + j]


def _pool_conv_sample(x4, pstate_t, cstate_t, maps, pscale, conv_w, layer):
    _, seq_len, n_seq, width = x4.shape
    n_buf = pstate_t.shape[0]
    taps = conv_w.shape[1]
    n_groups, gw = maps.shape[1], maps.shape[2]
    assert len(POOL_WINDOWS) == n_groups and n_buf >= max(POOL_WINDOWS) - 1

    return pl.pallas_call(
        functools.partial(_pool_conv_sample_kernel, windows=POOL_WINDOWS, start_pos=PAST_LEN),
        grid=(n_groups,),
        in_specs=[pl.BlockSpec((4, seq_len, n_seq, gw), lambda g: (0, 0, 0, g)),
                  pl.BlockSpec((n_buf, n_seq, gw), lambda g: (0, 0, g)),
                  pl.BlockSpec((taps - 1, n_seq, gw), lambda g: (0, 0, g)),
                  pl.BlockSpec((None, 1, gw, gw), lambda g: (layer, g, 0, 0)),
                  pl.BlockSpec((None, 1, gw), lambda g: (layer, 0, g)),
                  pl.BlockSpec((None, taps, gw), lambda g: (layer, 0, g))],
        out_specs=[pl.BlockSpec((seq_len, n_seq, gw), lambda g: (0, 0, g)),
                   pl.BlockSpec((seq_len, n_seq, gw), lambda g: (0, 0, g)),
                   pl.BlockSpec((n_buf, n_seq, gw), lambda g: (0, 0, g)),
                   pl.BlockSpec((taps - 1, n_seq, gw), lambda g: (0, 0, g))],
        out_shape=[jax.ShapeDtypeStruct((seq_len, n_seq, width), BF16),
                   jax.ShapeDtypeStruct((seq_len, n_seq, width), BF16),
                   jax.ShapeDtypeStruct(pstate_t.shape, F32),
                   jax.ShapeDtypeStruct(cstate_t.shape, F32)],
        compiler_params=_params(1),
        name="pool_conv_sample",
    )(x4, pstate_t, cstate_t, maps, pscale, conv_w)


def kernel(x_prompt, x_sample, state_C, state_n, state_m, state_pool, state_conv, w_in, b_in, mh_g, pool_maps, pool_scale, conv_w, w_bm, w_bp, w_bs, w_o, ln1_g, ln1_b, w_gate, w_up, w_down, ln2_g, ln2_b):
    depth = w_in.shape[0]
    batch, seq, d = x_prompt.shape
    n_seq, seq_len, _ = x_sample.shape
    n_heads, dv, dqk = state_C.shape[2], state_C.shape[3], state_C.shape[4]
    n_buf, pw = state_pool.shape[2], state_pool.shape[3]
    cw = conv_w.shape[2]
    dff = w_gate.shape[2]
    assert pw == cw and dv == 2 * dqk and 2 * n_heads <= LANES
    alpha = (2 * depth) ** 0.25
    tp, ts = batch * seq, n_seq * seq_len
    t_all = tp + ts

    off_i = 2 * n_heads * dqk + 2 * n_heads * dv
    off_p = off_i + 2 * n_heads
    gate_col0 = pw + 3 * cw
    k_blocks = 2 if dff % (2 * LANES) == 0 else 1

    w_in_t = jnp.swapaxes(w_in, 1, 2)
    w_if = jnp.pad(w_in[:, :, off_i:off_p], ((0, 0), (0, 0), (0, LANES - (off_p - off_i)))).astype(BF16)
    b3 = b_in[:, None, :]
    b_qkvo, b_tail = b3[:, :, :off_i], b3[:, :, off_p:]
    b_if = jnp.pad(b3[:, :, off_i:off_p], ((0, 0), (0, 0), (0, LANES - (off_p - off_i))))
    wbm, wbp, wbs = (_to_bf16(w) for w in (w_bm, w_bp, w_bs))
    maps = pool_maps.astype(BF16)
    mhg3, pscale3 = mh_g[:, None, :], pool_scale[:, None, :]
    ln1g, ln1b, ln2g, ln2b = (a[:, None, :] for a in (ln1_g, ln1_b, ln2_g, ln2_b))
    state_n5 = state_n[:, :, :, None, :]
    pool_t = state_pool.transpose(0, 2, 1, 3)
    conv_t = state_conv.transpose(0, 2, 1, 3)

    x, xb = _stack_rows(x_prompt.reshape(tp, d), x_sample.reshape(ts, d))

    outs = [[] for _ in range(9)]
    c_s = None
    for l in range(depth):
        qkvo = _matmul_nt(xb, w_in_t, l, 0, off_i, b_qkvo, out_dtype=BF16, name="in_qkvo")
        gates = _matmul(xb, w_if, l, bias=b_if, name="in_if")
        tail = _matmul_nt(xb, w_in_t, l, off_p, w_in.shape[2] - off_p, b_tail, name="in_tail")

        hm, c_p, n_p, m_p = _mlstm_prompt(qkvo, gates, mhg3, l, batch, seq, t_all, n_heads, dqk, dv)
        m_rep = jnp.repeat(state_m[l], seq_len, axis=0)
        hm, c_s, n_s, m_s = _mlstm_sample(qkvo, gates, mhg3, state_C, state_n5, m_rep, hm, c_s, l,
                                          tp, n_seq, seq_len, n_heads, dqk, dv)

        zp, zs, ps_p, cs_p = _pool_conv_prompt(tail, maps, pscale3, conv_w, l, batch, seq, pw, n_buf)
        x4 = tail[tp:, :pw + 3 * cw].reshape(n_seq, seq_len, 4, pw).transpose(2, 1, 0, 3)
        zp_s, zs_s, ps_s, cs_s = _pool_conv_sample(x4, pool_t[l], conv_t[l], maps, pscale3, conv_w, l)
        zp = lax.dynamic_update_slice(zp, zp_s.transpose(1, 0, 2).reshape(ts, pw), (tp, 0))
        zs = lax.dynamic_update_slice(zs, zs_s.transpose(1, 0, 2).reshape(ts, cw), (tp, 0))

        mixed = _merge(hm, zp, zs, wbm, wbp, wbs, l, tail, gate_col0)
        r1 = _matmul(mixed, w_o, l, res=x, res_scale=alpha, name="out_proj")
        x1, x1b = _layernorm(r1, ln1g, ln1b, l)

        ff = _ffn_up(x1b, w_gate, w_up, l, tn=FF_TILE)
        r2 = _matmul(ff, w_down, l, res=x1, res_scale=alpha, tn=FF_TILE, k_blocks=k_blocks,
                     name="ffn_down")
        for kb in range(1, k_blocks):
            r2 = _matmul(ff, w_down, l, res=r2, tn=FF_TILE, k_blocks=k_blocks, k_index=kb,
                         name="ffn_down")
        if l + 1 < depth:
            x, xb = _layernorm(r2, ln2g, ln2b, l)
        else:
            y_p, = _layernorm(r2, ln2g, ln2b, l, 0, tp, with_bf16=False)
            y_s, = _layernorm(r2, ln2g, ln2b, l, tp, ts, with_bf16=False)

        new = (c_p, n_p, m_p[:, :, 0], ps_p, cs_p,
               n_s[:, :, 0, :], m_s[:, seq_len - 1::seq_len, 0].T,
               ps_s.transpose(1, 0, 2), cs_s.transpose(1, 0, 2))
        for acc, val in zip(outs, new):
            acc.append(val)

    st = [jnp.stack(o) for o in outs]
    return (y_p.reshape(batch, seq, d), y_s.reshape(n_seq, seq_len, d),
            st[0], st[1], st[2], st[3], st[4], c_s, st[5], st[6], st[7], st[8])
```

```python
import functools

import jax
import jax.numpy as jnp
from jax import lax
from jax.experimental import pallas as pl
from jax.experimental.pallas import tpu as pltpu

F32 = jnp.float32
BF16 = jnp.bfloat16

POOL_WINDOWS = (2, 4, 8, 16)
PAST_LEN = 16384
LN_EPS = 1e-5
PROMPT_CHUNK = 128

LANES = 128
BF16_SUBLANES = 16
VMEM_LIMIT = 56 * 1024 * 1024
POOL_HALO = 16
CONV_HALO = 8
HEAD_GROUP = 2
SAMPLE_SEQS = 16
FF_TILE = 256


def _params(n_axes):
    return pltpu.CompilerParams(dimension_semantics=("arbitrary",) * n_axes,
                                vmem_limit_bytes=VMEM_LIMIT)


def _tile(dim, target, mult):
    best = None
    for d in range(mult, min(dim, target) + 1, mult):
        if dim % d == 0:
            best = d
    return dim if best is None else best


def _log_sigmoid(x):
    return jnp.minimum(x, 0.0) - jnp.log(1.0 + jnp.exp(-jnp.abs(x)))


def _cast_kernel(x_ref, o_ref):
    o_ref[...] = x_ref[...].astype(o_ref.dtype)


def _to_bf16(w):
    layers, rows, cols = w.shape
    tc = _tile(cols, 4096, LANES)
    tr = _tile(rows, max(BF16_SUBLANES, 1024 * 1024 // tc), BF16_SUBLANES)
    spec = pl.BlockSpec((None, tr, tc), lambda l, i, j: (l, i, j))
    return pl.pallas_call(
        _cast_kernel,
        grid=(layers, rows // tr, cols // tc),
        in_specs=[spec],
        out_specs=spec,
        out_shape=jax.ShapeDtypeStruct((layers, rows, cols), BF16),
        compiler_params=_params(3),
        name="to_bf16",
    )(w)


def _stack_kernel(xp_ref, xs_ref, of_ref, ob_ref, *, prompt_tiles):
    x = jnp.where(pl.program_id(0) < prompt_tiles, xp_ref[...], xs_ref[...])
    of_ref[...] = x
    ob_ref[...] = x.astype(BF16)


def _stack_rows(xp, xs):
    tp, d = xp.shape
    seq_len, n_seq, _ = xs.shape
    tr = n_seq
    assert tp % tr == 0 and tr % BF16_SUBLANES == 0
    n_p = tp // tr
    out = pl.BlockSpec((tr, d), lambda i: (i, 0))
    return pl.pallas_call(
        functools.partial(_stack_kernel, prompt_tiles=n_p),
        grid=(n_p + seq_len,),
        in_specs=[pl.BlockSpec((tr, d), lambda i: (jnp.minimum(i, n_p - 1), 0)),
                  pl.BlockSpec((None, n_seq, d), lambda i: (jnp.maximum(i - n_p, 0), 0, 0))],
        out_specs=[out, out],
        out_shape=[jax.ShapeDtypeStruct((tp + n_seq * seq_len, d), F32),
                   jax.ShapeDtypeStruct((tp + n_seq * seq_len, d), BF16)],
        compiler_params=_params(1),
        name="stack_rows",
    )(xp, xs)


def _mm_kernel(*refs, has_bias, has_res, res_scale):
    x_ref, w_ref = refs[0], refs[1]
    o_ref = refs[-1]
    acc = jnp.dot(x_ref[...], w_ref[...].astype(BF16), preferred_element_type=F32)
    pos = 2
    if has_bias:
        acc = acc + refs[pos][...]
        pos += 1
    if has_res:
        acc = acc + res_scale * refs[pos][...]
    o_ref[...] = acc.astype(o_ref.dtype)


def _matmul(x, w, layer, bias=None, res=None, res_scale=1.0, out_dtype=F32, tm=1088, tn=512,
            k_blocks=1, k_index=0, name="matmul"):
    t = x.shape[0]
    n = w.shape[2]
    k = x.shape[1] // k_blocks
    tm = _tile(t, tm, BF16_SUBLANES)
    tn = _tile(n, tn, LANES)
    in_specs = [pl.BlockSpec((tm, k), lambda i, j: (i, k_index)),
                pl.BlockSpec((None, k, tn), lambda i, j: (layer, k_index, j))]
    args = [x, w]
    if bias is not None:
        in_specs.append(pl.BlockSpec((None, 1, tn), lambda i, j: (layer, 0, j)))
        args.append(bias)
    if res is not None:
        in_specs.append(pl.BlockSpec((tm, tn), lambda i, j: (i, j)))
        args.append(res)
    return pl.pallas_call(
        functools.partial(_mm_kernel, has_bias=bias is not None, has_res=res is not None,
                          res_scale=res_scale),
        grid=(t // tm, n // tn),
        in_specs=in_specs,
        out_specs=pl.BlockSpec((tm, tn), lambda i, j: (i, j)),
        out_shape=jax.ShapeDtypeStruct((t, n), out_dtype),
        compiler_params=_params(2),
        name=name,
    )(*args)


def _mm_nt_kernel(x_ref, a_ref, *rest, shift):
    o_ref, b_ref = rest[-1], rest[-2]
    w = a_ref[...].astype(BF16)
    if shift:
        w = jnp.concatenate([w[shift:], rest[0][...].astype(BF16)], axis=0)
    acc = lax.dot_general(x_ref[...], w, (((1,), (1,)), ((), ())), preferred_element_type=F32)
    o_ref[...] = (acc + b_ref[...]).astype(o_ref.dtype)


def _matmul_nt(x, wt, layer, row0, n, bias, out_dtype=F32, tm=2176, tn=512, name="matmul_nt"):
    t, k = x.shape
    tm = _tile(t, tm, BF16_SUBLANES)
    tn = _tile(n, tn, LANES)
    shift = row0 % tn
    base = row0 - shift
    assert shift % BF16_SUBLANES == 0 and (shift == 0 or tn % shift == 0)
    in_specs = [pl.BlockSpec((tm, k), lambda i, j: (i, 0), pipeline_mode=pl.Buffered(1)),
                pl.BlockSpec((None, tn, k), lambda i, j: (layer, base // tn + j, 0))]
    args = [x, wt]
    if shift:
        in_specs.append(pl.BlockSpec((None, shift, k),
                                     lambda i, j: (layer, (base + (j + 1) * tn) // shift, 0)))
        args.append(wt)
    in_specs.append(pl.BlockSpec((None, 1, tn), lambda i, j: (layer, 0, j)))
    args.append(bias)
    return pl.pallas_call(
        functools.partial(_mm_nt_kernel, shift=shift),
        grid=(t // tm, n // tn),
        in_specs=in_specs,
        out_specs=pl.BlockSpec((tm, tn), lambda i, j: (i, j)),
        out_shape=jax.ShapeDtypeStruct((t, n), out_dtype),
        compiler_params=_params(2),
        name=name,
    )(*args)


def _ln_kernel(r_ref, g_ref, b_ref, of_ref, *maybe_bf16_ref):
    r = r_ref[...]
    mu = jnp.mean(r, axis=1, keepdims=True)
    d = r - mu
    var = jnp.mean(d * d, axis=1, keepdims=True)
    y = d * lax.rsqrt(var + LN_EPS) * g_ref[...] + b_ref[...]
    of_ref[...] = y
    for ob_ref in maybe_bf16_ref:
        ob_ref[...] = y.astype(BF16)


def _layernorm(r, g, b, layer, row0=0, n_rows=None, with_bf16=True, time_major_seqs=None):
    d = r.shape[1]
    n_rows = r.shape[0] - row0 if n_rows is None else n_rows
    tr = BF16_SUBLANES
    for cand in range(BF16_SUBLANES, 544 + 1, BF16_SUBLANES):
        if row0 % cand == 0 and n_rows % cand == 0:
            tr = cand
    out_row = pl.BlockSpec((tr, d), lambda i: (i, 0))
    out_shape = [jax.ShapeDtypeStruct((n_rows, d), F32)]
    if time_major_seqs is not None:
        tr = time_major_seqs
        assert row0 % tr == 0 and n_rows % tr == 0 and not with_bf16
        out_row = pl.BlockSpec((None, tr, d), lambda i: (i, 0, 0))
        out_shape = [jax.ShapeDtypeStruct((n_rows // tr, tr, d), F32)]
    blk0 = row0 // tr
    vec = pl.BlockSpec((None, 1, d), lambda i: (layer, 0, 0))
    out_specs = [out_row]
    if with_bf16:
        out_specs.append(out_row)
        out_shape.append(jax.ShapeDtypeStruct((n_rows, d), BF16))
    return pl.pallas_call(
        _ln_kernel,
        grid=(n_rows // tr,),
        in_specs=[pl.BlockSpec((tr, d), lambda i: (blk0 + i, 0)), vec, vec],
        out_specs=out_specs,
        out_shape=out_shape,
        compiler_params=_params(1),
        name="layernorm",
    )(r, g, b)


def _ffn_up_kernel(x_ref, wg_ref, wu_ref, o_ref):
    x = x_ref[...]
    a = jnp.dot(x, wg_ref[...].astype(BF16), preferred_element_type=F32)
    u = jnp.dot(x, wu_ref[...].astype(BF16), preferred_element_type=F32)
    o_ref[...] = (a * jax.nn.sigmoid(a) * u).astype(o_ref.dtype)


def _ffn_up(x, wg, wu, layer, tm=2176, tn=512):
    t, k = x.shape
    n = wg.shape[2]
    tm = _tile(t, tm, BF16_SUBLANES)
    tn = _tile(n, tn, LANES)
    wspec = pl.BlockSpec((None, k, tn), lambda i, j: (layer, 0, j))
    return pl.pallas_call(
        _ffn_up_kernel,
        grid=(t // tm, n // tn),
        in_specs=[pl.BlockSpec((tm, k), lambda i, j: (i, 0), pipeline_mode=pl.Buffered(1)),
                  wspec, wspec],
        out_specs=pl.BlockSpec((tm, tn), lambda i, j: (i, j)),
        out_shape=jax.ShapeDtypeStruct((t, n), BF16),
        compiler_params=_params(2),
        name="ffn_up",
    )(x, wg, wu)


def _merge_kernel(hm_ref, zp_ref, zs_ref, wm_ref, wp_ref, ws_ref, gm_ref, gp_ref, gs_ref, o_ref):
    ym = jnp.dot(hm_ref[...], wm_ref[...], preferred_element_type=F32)
    yp = jnp.dot(zp_ref[...], wp_ref[...], preferred_element_type=F32)
    ys = jnp.dot(zs_ref[...], ws_ref[...], preferred_element_type=F32)
    mixed = (jax.nn.sigmoid(gm_ref[...]) * ym + jax.nn.sigmoid(gp_ref[...]) * yp
             + jax.nn.sigmoid(gs_ref[...]) * ys)
    o_ref[...] = mixed.astype(o_ref.dtype)


def _merge(hm, zp, zs, wm, wp, ws, layer, tail, gate_col0, tm=544, tn=512):
    t = hm.shape[0]
    d = wm.shape[2]
    tm = _tile(t, tm, BF16_SUBLANES)
    tn = _tile(d, tn, LANES)
    g0 = gate_col0 // tn
    gd = d // tn

    def xspec(a):
        return pl.BlockSpec((tm, a.shape[1]), lambda i, j: (i, 0))

    def wspec(a):
        return pl.BlockSpec((None, a.shape[1], tn), lambda i, j: (layer, 0, j))

    def gspec(branch):
        return pl.BlockSpec((tm, tn), lambda i, j: (i, g0 + branch * gd + j))

    return pl.pallas_call(
        _merge_kernel,
        grid=(t // tm, d // tn),
        in_specs=[xspec(hm), xspec(zp), xspec(zs), wspec(wm), wspec(wp), wspec(ws),
                  gspec(0), gspec(1), gspec(2)],
        out_specs=pl.BlockSpec((tm, tn), lambda i, j: (i, j)),
        out_shape=jax.ShapeDtypeStruct((t, d), BF16),
        compiler_params=_params(2),
        name="merge",
    )(hm, zp, zs, wm, wp, ws, tail, tail, tail)


def _pick_lane(x, lane_index):
    lane = lax.broadcasted_iota(jnp.int32, x.shape, 1)
    return jnp.sum(jnp.where(lane == lane_index, x, 0.0), axis=1, keepdims=True)


def _col_to_row(col, eye):
    return jnp.sum(jnp.where(eye, col, 0.0), axis=0, keepdims=True)


def _segment_cumsum(x, seg_pos, seg_len, row_step=1):
    s = 1
    while s < seg_len:
        x = x + jnp.where(seg_pos >= s, pltpu.roll(x, s * row_step, axis=0), 0.0)
        s *= 2
    return x


def _head_norm_gate(hh, o_pre, mh_g):
    mu = jnp.mean(hh, axis=1, keepdims=True)
    d = hh - mu
    var = jnp.mean(d * d, axis=1, keepdims=True)
    return (jax.nn.sigmoid(o_pre.astype(F32)) * (d * lax.rsqrt(var + LN_EPS) * mh_g)).astype(BF16)


def _mlstm_prompt_kernel(q_ref, k_ref, v_ref, o_ref, g_ref, mhg_ref,
                         hm_ref, c_out, n_out, m_out,
                         ct_s, n_s, m_s, *, n_heads, dqk, dv, scale):
    c = pl.program_id(1)
    last = pl.num_programs(1) - 1
    L = q_ref.shape[0]

    @pl.when(c == 0)
    def _():
        ct_s[...] = jnp.zeros_like(ct_s)
        n_s[...] = jnp.zeros_like(n_s)
        m_s[...] = jnp.zeros_like(m_s)

    g = g_ref[...]
    row = lax.broadcasted_iota(jnp.int32, g.shape, 0)
    lane = lax.broadcasted_iota(jnp.int32, g.shape, 1)
    bsum = _segment_cumsum(_log_sigmoid(g), row, L)
    gb = jnp.where(lane < n_heads, g, bsum)
    gb_t = gb.T
    r2 = lax.broadcasted_iota(jnp.int32, (L, L), 0)
    c2 = lax.broadcasted_iota(jnp.int32, (L, L), 1)
    causal = r2 >= c2

    for h0 in range(0, n_heads, HEAD_GROUP):
        heads = range(h0, min(h0 + HEAD_GROUP, n_heads))
        st = {}
        for h in heads:
            logi_col = gb[:, h:h + 1]
            b_col = gb[:, n_heads + h:n_heads + h + 1]
            logi_row = gb_t[h:h + 1, :]
            b_row = gb_t[n_heads + h:n_heads + h + 1, :]
            dmat = jnp.where(causal, b_col - b_row + logi_row, -jnp.inf)
            inter = b_col + m_s[h:h + 1, :]
            m_t = jnp.maximum(inter, jnp.max(dmat, axis=1, keepdims=True))
            st[h] = dict(logi_col=logi_col, b_col=b_col, inter=inter, m_t=m_t,
                         w_inter=jnp.exp(inter - m_t) * scale,
                         decay=jnp.exp(dmat - m_t))
        for h in heads:
            q = q_ref[:, h * dqk:(h + 1) * dqk]
            k = k_ref[:, h * dqk:(h + 1) * dqk]
            ct = ct_s[h]
            qk = lax.dot_general(q, k, (((1,), (1,)), ((), ())), preferred_element_type=F32)
            st[h]["s"] = qk * scale * st[h]["decay"]
            st[h]["qc"] = jnp.dot(q, ct.astype(BF16), preferred_element_type=F32)
            st[h]["qn"] = jnp.sum(q.astype(F32) * n_s[h:h + 1, :], axis=1, keepdims=True)
        for h in heads:
            e = st[h]
            v = v_ref[:, h * dv:(h + 1) * dv]
            num = e["w_inter"] * e["qc"] + jnp.dot(e["s"].astype(BF16), v,
                                                   preferred_element_type=F32)
            den = e["w_inter"] * e["qn"] + jnp.sum(e["s"], axis=1, keepdims=True)
            hh = num * (1.0 / jnp.maximum(jnp.abs(den), jnp.exp(-e["m_t"])))
            hm_ref[:, h * dv:(h + 1) * dv] = _head_norm_gate(
                hh, o_ref[:, h * dv:(h + 1) * dv], mhg_ref[:, h * dv:(h + 1) * dv])
        for h in heads:
            e = st[h]
            k = k_ref[:, h * dqk:(h + 1) * dqk]
            v = v_ref[:, h * dv:(h + 1) * dv]
            m_new = e["m_t"][L - 1:L, :]
            w_c = jnp.exp(e["inter"][L - 1:L, :] - m_new)
            w_s = jnp.exp(e["b_col"][L - 1:L, :] - e["b_col"] + e["logi_col"] - m_new)
            vs = (w_s * v.astype(F32)).astype(BF16)
            ct_s[h] = w_c * ct_s[h] + lax.dot_general(k, vs, (((0,), (0,)), ((), ())),
                                                      preferred_element_type=F32)
            n_s[h:h + 1, :] = (w_c * n_s[h:h + 1, :]
                               + jnp.sum(w_s * k.astype(F32), axis=0, keepdims=True))
            m_s[h:h + 1, :] = m_new

    @pl.when(c == last)
    def _():
        for h in range(n_heads):
            c_out[0, h] = ct_s[h].T
        n_out[0] = n_s[...]
        m_out[0] = m_s[...]


def _mlstm_prompt(qkvo, gates, mh_g, layer, batch, seq, n_rows, n_heads, dqk, dv):
    L = PROMPT_CHUNK if seq % PROMPT_CHUNK == 0 else seq
    nc = seq // L
    qw, vw = n_heads * dqk, n_heads * dv
    assert qw % LANES == 0 and 2 * qw % vw == 0

    def rows(b, c):
        return b * nc + c

    return pl.pallas_call(
        functools.partial(_mlstm_prompt_kernel, n_heads=n_heads, dqk=dqk, dv=dv,
                          scale=dqk ** -0.5),
        grid=(batch, nc),
        in_specs=[
            pl.BlockSpec((L, qw), lambda b, c: (rows(b, c), 0)),
            pl.BlockSpec((L, qw), lambda b, c: (rows(b, c), 1)),
            pl.BlockSpec((L, vw), lambda b, c: (rows(b, c), 2 * qw // vw)),
            pl.BlockSpec((L, vw), lambda b, c: (rows(b, c), 2 * qw // vw + 1)),
            pl.BlockSpec((L, LANES), lambda b, c: (rows(b, c), 0)),
            pl.BlockSpec((None, 1, vw), lambda b, c: (layer, 0, 0)),
        ],
        out_specs=[
            pl.BlockSpec((L, vw), lambda b, c: (rows(b, c), 0)),
            pl.BlockSpec((1, n_heads, dv, dqk), lambda b, c: (b, 0, 0, 0)),
            pl.BlockSpec((1, n_heads, dqk), lambda b, c: (b, 0, 0)),
            pl.BlockSpec((1, n_heads, 1), lambda b, c: (b, 0, 0)),
        ],
        out_shape=[
            jax.ShapeDtypeStruct((n_rows, vw), BF16),
            jax.ShapeDtypeStruct((batch, n_heads, dv, dqk), F32),
            jax.ShapeDtypeStruct((batch, n_heads, dqk), F32),
            jax.ShapeDtypeStruct((batch, n_heads, 1), F32),
        ],
        scratch_shapes=[pltpu.VMEM((n_heads, dqk, dv), F32), pltpu.VMEM((n_heads, dqk), F32),
                        pltpu.VMEM((n_heads, 1), F32)],
        compiler_params=_params(2),
        name="mlstm_prompt",
    )(qkvo, qkvo, qkvo, qkvo, gates, mh_g)


def _mlstm_sample_kernel(q_ref, k_ref, v_ref, o_ref, g_ref, mhg_ref, c_ref, n_ref, mrep_ref,
                         *rest, n_heads, scale):
    hm_ref, c_out, n_out, m_out = rest[-4:]
    h = pl.program_id(1)
    seq_len, nb = q_ref.shape[0], q_ref.shape[1]
    R = seq_len * nb

    def rows(ref):
        return jnp.concatenate([ref[t] for t in range(seq_len)], axis=0)

    g = rows(g_ref)
    row = lax.broadcasted_iota(jnp.int32, g.shape, 0)
    bsum = _segment_cumsum(_log_sigmoid(g), row // nb, seq_len, row_step=nb)
    logi_col = _pick_lane(g, h)
    b_col = _pick_lane(bsum, h + n_heads)
    m_col = _pick_lane(rows(mrep_ref), h)

    r2 = lax.broadcasted_iota(jnp.int32, (R, R), 0)
    c2 = lax.broadcasted_iota(jnp.int32, (R, R), 1)
    eye = r2 == c2
    same = (r2 % nb) == (c2 % nb)
    b_row = _col_to_row(b_col, eye)
    logi_row = _col_to_row(logi_col, eye)
    dmat = jnp.where(same & (c2 <= r2), b_col - b_row + logi_row, -jnp.inf)
    inter = b_col + m_col
    m_t = jnp.maximum(inter, jnp.max(dmat, axis=1, keepdims=True))
    w_inter = jnp.exp(inter - m_t) * scale

    q = rows(q_ref)
    k = rows(k_ref)
    v = rows(v_ref)
    q32 = q.astype(F32)
    qk = lax.dot_general(q, k, (((1,), (1,)), ((), ())), preferred_element_type=F32) * scale
    s = qk * jnp.exp(dmat - m_t)
    num_s = jnp.dot(s.astype(BF16), v, preferred_element_type=F32)

    seq_of_row = lax.broadcasted_iota(jnp.int32, (R, 1), 0) % nb
    num_c = jnp.zeros(num_s.shape, F32)
    qn = jnp.zeros((R, 1), F32)
    for j in range(nb):
        mine = seq_of_row == j
        cq = lax.dot_general(q, c_ref[j, 0].astype(BF16), (((1,), (1,)), ((), ())),
                             preferred_element_type=F32)
        num_c = jnp.where(mine, cq, num_c)
        qn = jnp.where(mine, jnp.sum(q32 * n_ref[j, 0], axis=1, keepdims=True), qn)
    num = w_inter * num_c + num_s
    den = w_inter * qn + jnp.sum(s, axis=1, keepdims=True)
    hh = num * (1.0 / jnp.maximum(jnp.abs(den), jnp.exp(-m_t)))
    hm = _head_norm_gate(hh, rows(o_ref), mhg_ref[...])
    for t in range(seq_len):
        hm_ref[t] = hm[t * nb:(t + 1) * nb]

    pick_last = same & (c2 // nb == seq_len - 1)

    def last_of_seq(col):
        return jnp.sum(jnp.where(pick_last, _col_to_row(col, eye), 0.0), axis=1, keepdims=True)

    m_new = last_of_seq(m_t)
    w_c = jnp.exp(last_of_seq(inter) - m_new)
    w_s = jnp.exp(last_of_seq(b_col) - b_col + logi_col - m_new)
    vs = (w_s * v.astype(F32)).astype(BF16)
    kw = w_s * k.astype(F32)
    for j in range(nb):
        mine = seq_of_row == j
        upd = lax.dot_general(jnp.where(mine, vs, jnp.zeros_like(vs)), k,
                              (((0,), (0,)), ((), ())), preferred_element_type=F32)
        wc_j = w_c[j:j + 1, :]
        c_out[j, 0] = wc_j * c_ref[j, 0] + upd
        n_out[j, 0] = wc_j * n_ref[j, 0] + jnp.sum(jnp.where(mine, kw, 0.0), axis=0, keepdims=True)
    for t in range(seq_len):
        m_out[0, t] = m_new[t * nb:(t + 1) * nb]


def _mlstm_sample(qkvo_s, gates_s, mh_g, state_c, state_n, m_rep, c_prev, layer, n_heads, dqk, dv):
    seq_len, n_seq, _ = qkvo_s.shape
    nb = _tile(n_seq, SAMPLE_SEQS, BF16_SUBLANES)
    qw, vw = n_heads * dqk, n_heads * dv
    o_blk0 = (2 * qw + vw) // dv

    in_specs = [
        pl.BlockSpec((seq_len, nb, dqk), lambda i, h: (0, i, h)),
        pl.BlockSpec((seq_len, nb, dqk), lambda i, h: (0, i, n_heads + h)),
        pl.BlockSpec((seq_len, nb, dv), lambda i, h: (0, i, 2 * qw // dv + h)),
        pl.BlockSpec((seq_len, nb, dv), lambda i, h: (0, i, o_blk0 + h)),
        pl.BlockSpec((seq_len, nb, LANES), lambda i, h: (0, i, 0)),
        pl.BlockSpec((None, 1, dv), lambda i, h: (layer, 0, h)),
        pl.BlockSpec((None, nb, 1, dv, dqk), lambda i, h: (layer, i, h, 0, 0)),
        pl.BlockSpec((None, nb, 1, 1, dqk), lambda i, h: (layer, i, h, 0, 0)),
        pl.BlockSpec((seq_len, nb, n_heads), lambda i, h: (0, i, 0)),
    ]
    args = [qkvo_s, qkvo_s, qkvo_s, qkvo_s, gates_s, mh_g, state_c, state_n, m_rep]
    aliases = {}
    if c_prev is not None:
        in_specs.append(pl.BlockSpec(memory_space=pl.ANY))
        args.append(c_prev)
        aliases[len(args) - 1] = 1

    return pl.pallas_call(
        functools.partial(_mlstm_sample_kernel, n_heads=n_heads, scale=dqk ** -0.5),
        grid=(n_seq // nb, n_heads),
        in_specs=in_specs,
        out_specs=[
            pl.BlockSpec((seq_len, nb, dv), lambda i, h: (0, i, h)),
            pl.BlockSpec((None, nb, 1, dv, dqk), lambda i, h: (layer, i, h, 0, 0)),
            pl.BlockSpec((nb, 1, 1, dqk), lambda i, h: (i, h, 0, 0)),
            pl.BlockSpec((1, seq_len, nb, 1), lambda i, h: (h, 0, i, 0)),
        ],
        out_shape=[
            jax.ShapeDtypeStruct((seq_len, n_seq, vw), BF16),
            jax.ShapeDtypeStruct(state_c.shape, F32),
            jax.ShapeDtypeStruct(state_n.shape[1:], F32),
            jax.ShapeDtypeStruct((n_heads, seq_len, n_seq, 1), F32),
        ],
        input_output_aliases=aliases,
        compiler_params=_params(2),
        name="mlstm_sample",
    )(*args)


def _pool_conv_prompt_kernel(u_ref, uh_ref, cb_ref, cc_ref, cx_ref, cch_ref, cxh_ref,
                             maps_ref, scale_ref, cw_ref,
                             zp_ref, zs_ref, ps_ref, cs_ref, full_s, pf_s, *, windows):
    t = pl.program_id(1)
    last = pl.num_programs(1) - 1
    tr = u_ref.shape[0]
    gw = maps_ref.shape[1]
    n_buf = ps_ref.shape[1]
    taps = cw_ref.shape[0]
    has_past = t > 0

    full_s[0:POOL_HALO, :] = jnp.where(has_past, uh_ref[...], 0.0)
    full_s[POOL_HALO:, :] = u_ref[...]
    pf_s[0:CONV_HALO, :] = jnp.where(has_past, cch_ref[...] * cxh_ref[...], 0.0)
    pf_s[CONV_HALO:, :] = cc_ref[...] * cx_ref[...]
    pos = t * tr + lax.broadcasted_iota(jnp.int32, (tr, 1), 0)
    for g, w in enumerate(windows):
        cols = slice(g * gw, (g + 1) * gw)
        u = u_ref[:, cols]
        if w & (w - 1) == 0 and w <= POOL_HALO:
            win = full_s[:, cols]
            span = 1
            while span < w:
                win = win + pltpu.roll(win, span, axis=0)
                span *= 2
            acc = win[POOL_HALO:, :]
        else:
            acc = u
            for j in range(1, w):
                acc = acc + full_s[POOL_HALO - j:POOL_HALO - j + tr, cols]
        inv_cnt = 1.0 / jnp.minimum(pos + 1, w).astype(F32)
        zf = acc * inv_cnt - u
        zp = jnp.dot(zf.astype(BF16), maps_ref[g], preferred_element_type=F32) * scale_ref[:, cols]
        zp_ref[:, cols] = zp.astype(BF16)

        y = pf_s[CONV_HALO:CONV_HALO + tr, cols] * cw_ref[taps - 1:taps, cols]
        for j in range(taps - 1):
            back = taps - 1 - j
            y = y + pf_s[CONV_HALO - back:CONV_HALO - back + tr, cols] * cw_ref[j:j + 1, cols]
        zs_ref[:, cols] = (cb_ref[:, cols] * y).astype(BF16)

    @pl.when(t == last)
    def _():
        ps_ref[0] = full_s[POOL_HALO + tr - n_buf:POOL_HALO + tr, :]
        cs_ref[0] = pf_s[CONV_HALO + tr - (taps - 1):CONV_HALO + tr, :]


def _pool_conv_prompt(tail, maps, pscale, conv_w, layer, batch, seq, width, n_buf):
    tr = _tile(seq, 256, POOL_HALO)
    nt = seq // tr
    taps = conv_w.shape[1]
    n_groups, gw = maps.shape[1], maps.shape[2]
    hp = tr // POOL_HALO
    hc = tr // CONV_HALO

    def rows(b, t):
        return b * nt + t

    def main(col):
        return pl.BlockSpec((tr, width), lambda b, t: (rows(b, t), col))

    def halo(col, h, per_tile):
        return pl.BlockSpec((h, width), lambda b, t: (jnp.maximum(rows(b, t) * per_tile - 1, 0), col))

    return pl.pallas_call(
        functools.partial(_pool_conv_prompt_kernel, windows=POOL_WINDOWS),
        grid=(batch, nt),
        in_specs=[main(0), halo(0, POOL_HALO, hp), main(1), main(2), main(3),
                  halo(2, CONV_HALO, hc), halo(3, CONV_HALO, hc),
                  pl.BlockSpec((None, n_groups, gw, gw), lambda b, t: (layer, 0, 0, 0)),
                  pl.BlockSpec((None, 1, width), lambda b, t: (layer, 0, 0)),
                  pl.BlockSpec((None, taps, width), lambda b, t: (layer, 0, 0))],
        out_specs=[pl.BlockSpec((tr, width), lambda b, t: (rows(b, t), 0)),
                   pl.BlockSpec((tr, width), lambda b, t: (rows(b, t), 0)),
                   pl.BlockSpec((1, n_buf, width), lambda b, t: (b, 0, 0)),
                   pl.BlockSpec((1, taps - 1, width), lambda b, t: (b, 0, 0))],
        out_shape=[jax.ShapeDtypeStruct((tail.shape[0], width), BF16),
                   jax.ShapeDtypeStruct((tail.shape[0], width), BF16),
                   jax.ShapeDtypeStruct((batch, n_buf, width), F32),
                   jax.ShapeDtypeStruct((batch, taps - 1, width), F32)],
        scratch_shapes=[pltpu.VMEM((POOL_HALO + tr, width), F32),
                        pltpu.VMEM((CONV_HALO + tr, width), F32)],
        compiler_params=_params(2),
        name="pool_conv_prompt",
    )(tail, tail, tail, tail, tail, tail, tail, maps, pscale, conv_w)


def _pool_conv_sample_kernel(u_ref, cb_ref, cc_ref, cx_ref, ps_ref, cs_ref, maps_ref, scale_ref,
                             cw_ref, zp_ref, zs_ref, pn_ref, cn_ref, *, windows, start_pos):
    g = pl.program_id(0)
    seq_len = u_ref.shape[0]
    n_buf = ps_ref.shape[0]
    taps = cw_ref.shape[0]
    win = jnp.int32(windows[0])
    for i, w in enumerate(windows):
        win = jnp.where(g == i, jnp.int32(w), win)

    hist = [ps_ref[j] for j in range(n_buf)] + [u_ref[t] for t in range(seq_len)]
    for t in range(seq_len):
        cur = n_buf + t
        acc = hist[cur]
        for j in range(1, max(windows)):
            acc = acc + jnp.where(j < win, hist[cur - j], 0.0)
        cnt = jnp.minimum(start_pos + t + 1, win).astype(F32)
        zf = acc / cnt - hist[cur]
        zp = jnp.dot(zf.astype(BF16), maps_ref[0], preferred_element_type=F32) * scale_ref[...]
        zp_ref[t] = zp.astype(BF16)
    for j in range(n_buf):
        pn_ref[j] = hist[seq_len + j]

    prod = [cs_ref[j] for j in range(taps - 1)] + [cc_ref[t] * cx_ref[t] for t in range(seq_len)]
    for t in range(seq_len):
        y = prod[t] * cw_ref[0:1, :]
        for j in range(1, taps):
            y = y + prod[t + j] * cw_ref[j:j + 1, :]
        zs_ref[t] = (cb_ref[t] * y).astype(BF16)
    for j in range(taps - 1):
        cn_ref[j] = prod[seq_len + j]


def _pool_conv_sample(tail_s, pstate_t, cstate_t, maps, pscale, conv_w, layer):
    seq_len, n_seq, _ = tail_s.shape
    n_buf, width = pstate_t.shape[1], pstate_t.shape[3]
    taps = conv_w.shape[1]
    n_groups, gw = maps.shape[1], maps.shape[2]
    assert len(POOL_WINDOWS) == n_groups and n_buf >= max(POOL_WINDOWS) - 1

    def section(k):
        return pl.BlockSpec((seq_len, n_seq, gw), lambda g: (0, 0, k * n_groups + g))

    return pl.pallas_call(
        functools.partial(_pool_conv_sample_kernel, windows=POOL_WINDOWS, start_pos=PAST_LEN),
        grid=(n_groups,),
        in_specs=[section(0), section(1), section(2), section(3),
                  pl.BlockSpec((None, n_buf, n_seq, gw), lambda g: (layer, 0, 0, g)),
                  pl.BlockSpec((None, taps - 1, n_seq, gw), lambda g: (layer, 0, 0, g)),
                  pl.BlockSpec((None, 1, gw, gw), lambda g: (layer, g, 0, 0)),
                  pl.BlockSpec((None, 1, gw), lambda g: (layer, 0, g)),
                  pl.BlockSpec((None, taps, gw), lambda g: (layer, 0, g))],
        out_specs=[pl.BlockSpec((seq_len, n_seq, gw), lambda g: (0, 0, g)),
                   pl.BlockSpec((seq_len, n_seq, gw), lambda g: (0, 0, g)),
                   pl.BlockSpec((n_buf, n_seq, gw), lambda g: (0, 0, g)),
                   pl.BlockSpec((taps - 1, n_seq, gw), lambda g: (0, 0, g))],
        out_shape=[jax.ShapeDtypeStruct((seq_len, n_seq, width), BF16),
                   jax.ShapeDtypeStruct((seq_len, n_seq, width), BF16),
                   jax.ShapeDtypeStruct(pstate_t.shape[1:], F32),
                   jax.ShapeDtypeStruct(cstate_t.shape[1:], F32)],
        compiler_params=_params(1),
        name="pool_conv_sample",
    )(tail_s, tail_s, tail_s, tail_s, pstate_t, cstate_t, maps, pscale, conv_w)


def kernel(x_prompt, x_sample, state_C, state_n, state_m, state_pool, state_conv, w_in, b_in, mh_g, pool_maps, pool_scale, conv_w, w_bm, w_bp, w_bs, w_o, ln1_g, ln1_b, w_gate, w_up, w_down, ln2_g, ln2_b):
    depth = w_in.shape[0]
    batch, seq, d = x_prompt.shape
    n_seq, seq_len, _ = x_sample.shape
    n_heads, dv, dqk = state_C.shape[2], state_C.shape[3], state_C.shape[4]
    n_buf, pw = state_pool.shape[2], state_pool.shape[3]
    cw = conv_w.shape[2]
    dff = w_gate.shape[2]
    assert pw == cw and dv == 2 * dqk and 2 * n_heads <= LANES
    alpha = (2 * depth) ** 0.25
    tp, ts = batch * seq, n_seq * seq_len
    t_all = tp + ts

    off_i = 2 * n_heads * dqk + 2 * n_heads * dv
    off_p = off_i + 2 * n_heads
    gate_col0 = pw + 3 * cw
    k_blocks = 2 if dff % (2 * LANES) == 0 else 1

    w_in_t = jnp.swapaxes(w_in, 1, 2)
    w_if = jnp.pad(w_in[:, :, off_i:off_p], ((0, 0), (0, 0), (0, LANES - (off_p - off_i)))).astype(BF16)
    b3 = b_in[:, None, :]
    b_qkvo, b_tail = b3[:, :, :off_i], b3[:, :, off_p:]
    b_if = jnp.pad(b3[:, :, off_i:off_p], ((0, 0), (0, 0), (0, LANES - (off_p - off_i))))
    wbm, wbp, wbs = (_to_bf16(w) for w in (w_bm, w_bp, w_bs))
    maps = pool_maps.astype(BF16)
    mhg3, pscale3 = mh_g[:, None, :], pool_scale[:, None, :]
    ln1g, ln1b, ln2g, ln2b = (a[:, None, :] for a in (ln1_g, ln1_b, ln2_g, ln2_b))
    state_n5 = state_n[:, :, :, None, :]
    pool_t = state_pool.transpose(0, 2, 1, 3)
    conv_t = state_conv.transpose(0, 2, 1, 3)

    x, xb = _stack_rows(x_prompt.reshape(tp, d), x_sample.transpose(1, 0, 2))

    outs = [[] for _ in range(9)]
    c_s = None
    for l in range(depth):
        qkvo = _matmul_nt(xb, w_in_t, l, 0, off_i, b_qkvo, out_dtype=BF16, name="in_qkvo")
        gates = _matmul(xb, w_if, l, bias=b_if, name="in_if")
        tail = _matmul_nt(xb, w_in_t, l, off_p, w_in.shape[2] - off_p, b_tail, name="in_tail")

        hm, c_p, n_p, m_p = _mlstm_prompt(qkvo, gates, mhg3, l, batch, seq, t_all, n_heads, dqk, dv)
        m_rep = jnp.broadcast_to(state_m[l][None], (seq_len, n_seq, n_heads))
        hm_s, c_s, n_s, m_s = _mlstm_sample(
            qkvo[tp:].reshape(seq_len, n_seq, -1), gates[tp:].reshape(seq_len, n_seq, -1), mhg3,
            state_C, state_n5, m_rep, c_s, l, n_heads, dqk, dv)
        hm = lax.dynamic_update_slice(hm, hm_s.reshape(ts, -1), (tp, 0))

        zp, zs, ps_p, cs_p = _pool_conv_prompt(tail, maps, pscale3, conv_w, l, batch, seq, pw, n_buf)
        tail_s = tail[tp:, :pw + 3 * cw].reshape(seq_len, n_seq, -1)
        zp_s, zs_s, ps_s, cs_s = _pool_conv_sample(tail_s, pool_t, conv_t, maps, pscale3, conv_w, l)
        zp = lax.dynamic_update_slice(zp, zp_s.reshape(ts, pw), (tp, 0))
        zs = lax.dynamic_update_slice(zs, zs_s.reshape(ts, cw), (tp, 0))

        mixed = _merge(hm, zp, zs, wbm, wbp, wbs, l, tail, gate_col0)
        r1 = _matmul(mixed, w_o, l, res=x, res_scale=alpha, name="out_proj")
        x1, x1b = _layernorm(r1, ln1g, ln1b, l)

        ff = _ffn_up(x1b, w_gate, w_up, l, tn=FF_TILE)
        r2 = _matmul(ff, w_down, l, res=x1, res_scale=alpha, tn=FF_TILE, k_blocks=k_blocks,
                     name="ffn_down")
        for kb in range(1, k_blocks):
            r2 = _matmul(ff, w_down, l, res=r2, tn=FF_TILE, k_blocks=k_blocks, k_index=kb,
                         name="ffn_down")
        if l + 1 < depth:
            x, xb = _layernorm(r2, ln2g, ln2b, l)
        else:
            y_p, = _layernorm(r2, ln2g, ln2b, l, 0, tp, with_bf16=False)
            y_s, = _layernorm(r2, ln2g, ln2b, l, tp, ts, with_bf16=False, time_major_seqs=n_seq)

        new = (c_p, n_p, m_p[:, :, 0], ps_p, cs_p,
               n_s[:, :, 0, :], m_s[:, seq_len - 1, :, 0].T,
               ps_s.transpose(1, 0, 2), cs_s.transpose(1, 0, 2))
        for acc, val in zip(outs, new):
            acc.append(val)

    st = [jnp.stack(o) for o in outs]
    return (y_p.reshape(batch, seq, d), y_s.transpose(1, 0, 2),
            st[0], st[1], st[2], st[3], st[4], c_s, st[5], st[6], st[7], st[8])
```

```python
import functools

import jax
import jax.numpy as jnp
from jax import lax
from jax.experimental import pallas as pl
from jax.experimental.pallas import tpu as pltpu

F32 = jnp.float32
BF16 = jnp.bfloat16

POOL_WINDOWS = (2, 4, 8, 16)
PAST_LEN = 16384
LN_EPS = 1e-5
PROMPT_CHUNK = 256

LANES = 128
BF16_SUBLANES = 16
VMEM_LIMIT = 56 * 1024 * 1024
POOL_HALO = 16
CONV_HALO = 8
HEAD_GROUP = 2
SAMPLE_SEQS = 16
FF_TILE = 256


def _params(n_axes):
    return pltpu.CompilerParams(dimension_semantics=("arbitrary",) * n_axes,
                                vmem_limit_bytes=VMEM_LIMIT)


def _tile(dim, target, mult):
    best = None
    for d in range(mult, min(dim, target) + 1, mult):
        if dim % d == 0:
            best = d
    return dim if best is None else best


def _log_sigmoid(x):
    return jnp.minimum(x, 0.0) - jnp.log(1.0 + jnp.exp(-jnp.abs(x)))


def _cast_kernel(x_ref, o_ref):
    o_ref[...] = x_ref[...].astype(o_ref.dtype)


def _to_bf16(w):
    layers, rows, cols = w.shape
    tc = _tile(cols, 4096, LANES)
    tr = _tile(rows, max(BF16_SUBLANES, 1024 * 1024 // tc), BF16_SUBLANES)
    spec = pl.BlockSpec((None, tr, tc), lambda l, i, j: (l, i, j))
    return pl.pallas_call(
        _cast_kernel,
        grid=(layers, rows // tr, cols // tc),
        in_specs=[spec],
        out_specs=spec,
        out_shape=jax.ShapeDtypeStruct((layers, rows, cols), BF16),
        compiler_params=_params(3),
        name="to_bf16",
    )(w)


def _stack_kernel(xp_ref, xs_ref, of_ref, ob_ref, *, prompt_tiles):
    x = jnp.where(pl.program_id(0) < prompt_tiles, xp_ref[...], xs_ref[...])
    of_ref[...] = x
    ob_ref[...] = x.astype(BF16)


def _stack_rows(xp, xs):
    tp, d = xp.shape
    seq_len, n_seq, _ = xs.shape
    tr = n_seq
    assert tp % tr == 0 and tr % BF16_SUBLANES == 0
    n_p = tp // tr
    out = pl.BlockSpec((tr, d), lambda i: (i, 0))
    return pl.pallas_call(
        functools.partial(_stack_kernel, prompt_tiles=n_p),
        grid=(n_p + seq_len,),
        in_specs=[pl.BlockSpec((tr, d), lambda i: (jnp.minimum(i, n_p - 1), 0)),
                  pl.BlockSpec((None, n_seq, d), lambda i: (jnp.maximum(i - n_p, 0), 0, 0))],
        out_specs=[out, out],
        out_shape=[jax.ShapeDtypeStruct((tp + n_seq * seq_len, d), F32),
                   jax.ShapeDtypeStruct((tp + n_seq * seq_len, d), BF16)],
        compiler_params=_params(1),
        name="stack_rows",
    )(xp, xs)


def _mm_kernel(*refs, has_bias, has_res, res_scale):
    x_ref, w_ref = refs[0], refs[1]
    o_ref = refs[-1]
    acc = jnp.dot(x_ref[...], w_ref[...].astype(BF16), preferred_element_type=F32)
    pos = 2
    if has_bias:
        acc = acc + refs[pos][...]
        pos += 1
    if has_res:
        acc = acc + res_scale * refs[pos][...]
    o_ref[...] = acc.astype(o_ref.dtype)


def _matmul(x, w, layer, bias=None, res=None, res_scale=1.0, out_dtype=F32, tm=1088, tn=512,
            k_blocks=1, k_index=0, name="matmul"):
    t = x.shape[0]
    n = w.shape[2]
    k = x.shape[1] // k_blocks
    tm = _tile(t, tm, BF16_SUBLANES)
    tn = _tile(n, tn, LANES)
    in_specs = [pl.BlockSpec((tm, k), lambda i, j: (i, k_index)),
                pl.BlockSpec((None, k, tn), lambda i, j: (layer, k_index, j))]
    args = [x, w]
    if bias is not None:
        in_specs.append(pl.BlockSpec((None, 1, tn), lambda i, j: (layer, 0, j)))
        args.append(bias)
    if res is not None:
        in_specs.append(pl.BlockSpec((tm, tn), lambda i, j: (i, j)))
        args.append(res)
    return pl.pallas_call(
        functools.partial(_mm_kernel, has_bias=bias is not None, has_res=res is not None,
                          res_scale=res_scale),
        grid=(t // tm, n // tn),
        in_specs=in_specs,
        out_specs=pl.BlockSpec((tm, tn), lambda i, j: (i, j)),
        out_shape=jax.ShapeDtypeStruct((t, n), out_dtype),
        compiler_params=_params(2),
        name=name,
    )(*args)


def _mm_nt_kernel(x_ref, a_ref, *rest, shift):
    o_ref, b_ref = rest[-1], rest[-2]
    w = a_ref[...].astype(BF16)
    if shift:
        w = jnp.concatenate([w[shift:], rest[0][...].astype(BF16)], axis=0)
    acc = lax.dot_general(x_ref[...], w, (((1,), (1,)), ((), ())), preferred_element_type=F32)
    o_ref[...] = (acc + b_ref[...]).astype(o_ref.dtype)


def _matmul_nt(x, wt, layer, row0, n, bias, out_dtype=F32, tm=2176, tn=512, name="matmul_nt"):
    t, k = x.shape
    tm = _tile(t, tm, BF16_SUBLANES)
    tn = _tile(n, tn, LANES)
    shift = row0 % tn
    base = row0 - shift
    assert shift % BF16_SUBLANES == 0 and (shift == 0 or tn % shift == 0)
    in_specs = [pl.BlockSpec((tm, k), lambda i, j: (i, 0), pipeline_mode=pl.Buffered(1)),
                pl.BlockSpec((None, tn, k), lambda i, j: (layer, base // tn + j, 0))]
    args = [x, wt]
    if shift:
        in_specs.append(pl.BlockSpec((None, shift, k),
                                     lambda i, j: (layer, (base + (j + 1) * tn) // shift, 0)))
        args.append(wt)
    in_specs.append(pl.BlockSpec((None, 1, tn), lambda i, j: (layer, 0, j)))
    args.append(bias)
    return pl.pallas_call(
        functools.partial(_mm_nt_kernel, shift=shift),
        grid=(t // tm, n // tn),
        in_specs=in_specs,
        out_specs=pl.BlockSpec((tm, tn), lambda i, j: (i, j)),
        out_shape=jax.ShapeDtypeStruct((t, n), out_dtype),
        compiler_params=_params(2),
        name=name,
    )(*args)


def _ln_kernel(r_ref, g_ref, b_ref, of_ref, *maybe_bf16_ref):
    r = r_ref[...]
    mu = jnp.mean(r, axis=1, keepdims=True)
    d = r - mu
    var = jnp.mean(d * d, axis=1, keepdims=True)
    y = d * lax.rsqrt(var + LN_EPS) * g_ref[...] + b_ref[...]
    of_ref[...] = y
    for ob_ref in maybe_bf16_ref:
        ob_ref[...] = y.astype(BF16)


def _layernorm(r, g, b, layer, row0=0, n_rows=None, with_bf16=True, time_major_seqs=None):
    d = r.shape[1]
    n_rows = r.shape[0] - row0 if n_rows is None else n_rows
    tr = BF16_SUBLANES
    for cand in range(BF16_SUBLANES, 544 + 1, BF16_SUBLANES):
        if row0 % cand == 0 and n_rows % cand == 0:
            tr = cand
    out_row = pl.BlockSpec((tr, d), lambda i: (i, 0))
    out_shape = [jax.ShapeDtypeStruct((n_rows, d), F32)]
    if time_major_seqs is not None:
        tr = time_major_seqs
        assert row0 % tr == 0 and n_rows % tr == 0 and not with_bf16
        out_row = pl.BlockSpec((None, tr, d), lambda i: (i, 0, 0))
        out_shape = [jax.ShapeDtypeStruct((n_rows // tr, tr, d), F32)]
    blk0 = row0 // tr
    vec = pl.BlockSpec((None, 1, d), lambda i: (layer, 0, 0))
    out_specs = [out_row]
    if with_bf16:
        out_specs.append(out_row)
        out_shape.append(jax.ShapeDtypeStruct((n_rows, d), BF16))
    return pl.pallas_call(
        _ln_kernel,
        grid=(n_rows // tr,),
        in_specs=[pl.BlockSpec((tr, d), lambda i: (blk0 + i, 0)), vec, vec],
        out_specs=out_specs,
        out_shape=out_shape,
        compiler_params=_params(1),
        name="layernorm",
    )(r, g, b)


def _ffn_up_kernel(x_ref, wg_ref, wu_ref, o_ref):
    x = x_ref[...]
    a = jnp.dot(x, wg_ref[...].astype(BF16), preferred_element_type=F32)
    u = jnp.dot(x, wu_ref[...].astype(BF16), preferred_element_type=F32)
    o_ref[...] = (a * jax.nn.sigmoid(a) * u).astype(o_ref.dtype)


def _ffn_up(x, wg, wu, layer, tm=2176, tn=512):
    t, k = x.shape
    n = wg.shape[2]
    tm = _tile(t, tm, BF16_SUBLANES)
    tn = _tile(n, tn, LANES)
    wspec = pl.BlockSpec((None, k, tn), lambda i, j: (layer, 0, j))
    return pl.pallas_call(
        _ffn_up_kernel,
        grid=(t // tm, n // tn),
        in_specs=[pl.BlockSpec((tm, k), lambda i, j: (i, 0), pipeline_mode=pl.Buffered(1)),
                  wspec, wspec],
        out_specs=pl.BlockSpec((tm, tn), lambda i, j: (i, j)),
        out_shape=jax.ShapeDtypeStruct((t, n), BF16),
        compiler_params=_params(2),
        name="ffn_up",
    )(x, wg, wu)


def _merge_kernel(hm_ref, zp_ref, zs_ref, wm_ref, wp_ref, ws_ref, gm_ref, gp_ref, gs_ref, o_ref):
    ym = jnp.dot(hm_ref[...], wm_ref[...], preferred_element_type=F32)
    yp = jnp.dot(zp_ref[...], wp_ref[...], preferred_element_type=F32)
    ys = jnp.dot(zs_ref[...], ws_ref[...], preferred_element_type=F32)
    mixed = (jax.nn.sigmoid(gm_ref[...]) * ym + jax.nn.sigmoid(gp_ref[...]) * yp
             + jax.nn.sigmoid(gs_ref[...]) * ys)
    o_ref[...] = mixed.astype(o_ref.dtype)


def _merge(hm, zp, zs, wm, wp, ws, layer, tail, gate_col0, tm=544, tn=512):
    t = hm.shape[0]
    d = wm.shape[2]
    tm = _tile(t, tm, BF16_SUBLANES)
    tn = _tile(d, tn, LANES)
    g0 = gate_col0 // tn
    gd = d // tn

    def xspec(a):
        return pl.BlockSpec((tm, a.shape[1]), lambda i, j: (i, 0))

    def wspec(a):
        return pl.BlockSpec((None, a.shape[1], tn), lambda i, j: (layer, 0, j))

    def gspec(branch):
        return pl.BlockSpec((tm, tn), lambda i, j: (i, g0 + branch * gd + j))

    return pl.pallas_call(
        _merge_kernel,
        grid=(t // tm, d // tn),
        in_specs=[xspec(hm), xspec(zp), xspec(zs), wspec(wm), wspec(wp), wspec(ws),
                  gspec(0), gspec(1), gspec(2)],
        out_specs=pl.BlockSpec((tm, tn), lambda i, j: (i, j)),
        out_shape=jax.ShapeDtypeStruct((t, d), BF16),
        compiler_params=_params(2),
        name="merge",
    )(hm, zp, zs, wm, wp, ws, tail, tail, tail)


def _pick_lane(x, lane_index):
    lane = lax.broadcasted_iota(jnp.int32, x.shape, 1)
    return jnp.sum(jnp.where(lane == lane_index, x, 0.0), axis=1, keepdims=True)


def _col_to_row(col, eye):
    return jnp.sum(jnp.where(eye, col, 0.0), axis=0, keepdims=True)


def _segment_cumsum(x, seg_pos, seg_len, row_step=1):
    s = 1
    while s < seg_len:
        x = x + jnp.where(seg_pos >= s, pltpu.roll(x, s * row_step, axis=0), 0.0)
        s *= 2
    return x


def _head_norm_gate(hh, o_pre, mh_g):
    mu = jnp.mean(hh, axis=1, keepdims=True)
    d = hh - mu
    var = jnp.mean(d * d, axis=1, keepdims=True)
    return (jax.nn.sigmoid(o_pre.astype(F32)) * (d * lax.rsqrt(var + LN_EPS) * mh_g)).astype(BF16)


def _mlstm_prompt_kernel(q_ref, k_ref, v_ref, o_ref, g_ref, mhg_ref,
                         hm_ref, c_out, n_out, m_out,
                         ct_s, n_s, m_s, *, n_heads, dqk, dv, scale):
    c = pl.program_id(1)
    last = pl.num_programs(1) - 1
    L = q_ref.shape[0]

    @pl.when(c == 0)
    def _():
        ct_s[...] = jnp.zeros_like(ct_s)
        n_s[...] = jnp.zeros_like(n_s)
        m_s[...] = jnp.zeros_like(m_s)

    g = g_ref[...]
    row = lax.broadcasted_iota(jnp.int32, g.shape, 0)
    lane = lax.broadcasted_iota(jnp.int32, g.shape, 1)
    bsum = _segment_cumsum(_log_sigmoid(g), row, L)
    gb = jnp.where(lane < n_heads, g, bsum)
    gb_t = gb.T
    r2 = lax.broadcasted_iota(jnp.int32, (L, L), 0)
    c2 = lax.broadcasted_iota(jnp.int32, (L, L), 1)
    causal = r2 >= c2

    for h0 in range(0, n_heads, HEAD_GROUP):
        heads = range(h0, min(h0 + HEAD_GROUP, n_heads))
        st = {}
        for h in heads:
            logi_col = gb[:, h:h + 1]
            b_col = gb[:, n_heads + h:n_heads + h + 1]
            logi_row = gb_t[h:h + 1, :]
            b_row = gb_t[n_heads + h:n_heads + h + 1, :]
            dmat = jnp.where(causal, b_col - b_row + logi_row, -jnp.inf)
            inter = b_col + m_s[h:h + 1, :]
            m_t = jnp.maximum(inter, jnp.max(dmat, axis=1, keepdims=True))
            st[h] = dict(logi_col=logi_col, b_col=b_col, inter=inter, m_t=m_t,
                         w_inter=jnp.exp(inter - m_t) * scale,
                         decay=jnp.exp(dmat - m_t))
        for h in heads:
            q = q_ref[:, h * dqk:(h + 1) * dqk]
            k = k_ref[:, h * dqk:(h + 1) * dqk]
            ct = ct_s[h]
            qk = lax.dot_general(q, k, (((1,), (1,)), ((), ())), preferred_element_type=F32)
            st[h]["s"] = qk * scale * st[h]["decay"]
            st[h]["qc"] = jnp.dot(q, ct.astype(BF16), preferred_element_type=F32)
            st[h]["qn"] = jnp.sum(q.astype(F32) * n_s[h:h + 1, :], axis=1, keepdims=True)
        for h in heads:
            e = st[h]
            v = v_ref[:, h * dv:(h + 1) * dv]
            num = e["w_inter"] * e["qc"] + jnp.dot(e["s"].astype(BF16), v,
                                                   preferred_element_type=F32)
            den = e["w_inter"] * e["qn"] + jnp.sum(e["s"], axis=1, keepdims=True)
            hh = num * (1.0 / jnp.maximum(jnp.abs(den), jnp.exp(-e["m_t"])))
            hm_ref[:, h * dv:(h + 1) * dv] = _head_norm_gate(
                hh, o_ref[:, h * dv:(h + 1) * dv], mhg_ref[:, h * dv:(h + 1) * dv])
        for h in heads:
            e = st[h]
            k = k_ref[:, h * dqk:(h + 1) * dqk]
            v = v_ref[:, h * dv:(h + 1) * dv]
            m_new = e["m_t"][L - 1:L, :]
            w_c = jnp.exp(e["inter"][L - 1:L, :] - m_new)
            w_s = jnp.exp(e["b_col"][L - 1:L, :] - e["b_col"] + e["logi_col"] - m_new)
            vs = (w_s * v.astype(F32)).astype(BF16)
            ct_s[h] = w_c * ct_s[h] + lax.dot_general(k, vs, (((0,), (0,)), ((), ())),
                                                      preferred_element_type=F32)
            n_s[h:h + 1, :] = (w_c * n_s[h:h + 1, :]
                               + jnp.sum(w_s * k.astype(F32), axis=0, keepdims=True))
            m_s[h:h + 1, :] = m_new

    @pl.when(c == last)
    def _():
        for h in range(n_heads):
            c_out[0, h] = ct_s[h].T
        n_out[0] = n_s[...]
        m_out[0] = m_s[...]


def _mlstm_prompt(qkvo, gates, mh_g, layer, batch, seq, n_rows, n_heads, dqk, dv):
    L = PROMPT_CHUNK if seq % PROMPT_CHUNK == 0 else seq
    nc = seq // L
    qw, vw = n_heads * dqk, n_heads * dv
    assert qw % LANES == 0 and 2 * qw % vw == 0

    def rows(b, c):
        return b * nc + c

    return pl.pallas_call(
        functools.partial(_mlstm_prompt_kernel, n_heads=n_heads, dqk=dqk, dv=dv,
                          scale=dqk ** -0.5),
        grid=(batch, nc),
        in_specs=[
            pl.BlockSpec((L, qw), lambda b, c: (rows(b, c), 0)),
            pl.BlockSpec((L, qw), lambda b, c: (rows(b, c), 1)),
            pl.BlockSpec((L, vw), lambda b, c: (rows(b, c), 2 * qw // vw)),
            pl.BlockSpec((L, vw), lambda b, c: (rows(b, c), 2 * qw // vw + 1)),
            pl.BlockSpec((L, LANES), lambda b, c: (rows(b, c), 0)),
            pl.BlockSpec((None, 1, vw), lambda b, c: (layer, 0, 0)),
        ],
        out_specs=[
            pl.BlockSpec((L, vw), lambda b, c: (rows(b, c), 0)),
            pl.BlockSpec((1, n_heads, dv, dqk), lambda b, c: (b, 0, 0, 0)),
            pl.BlockSpec((1, n_heads, dqk), lambda b, c: (b, 0, 0)),
            pl.BlockSpec((1, n_heads, 1), lambda b, c: (b, 0, 0)),
        ],
        out_shape=[
            jax.ShapeDtypeStruct((n_rows, vw), BF16),
            jax.ShapeDtypeStruct((batch, n_heads, dv, dqk), F32),
            jax.ShapeDtypeStruct((batch, n_heads, dqk), F32),
            jax.ShapeDtypeStruct((batch, n_heads, 1), F32),
        ],
        scratch_shapes=[pltpu.VMEM((n_heads, dqk, dv), F32), pltpu.VMEM((n_heads, dqk), F32),
                        pltpu.VMEM((n_heads, 1), F32)],
        compiler_params=_params(2),
        name="mlstm_prompt",
    )(qkvo, qkvo, qkvo, qkvo, gates, mh_g)


def _mlstm_sample_kernel(q_ref, k_ref, v_ref, o_ref, g_ref, mhg_ref, c_ref, n_ref, mrep_ref,
                         *rest, n_heads, scale):
    hm_ref, c_out, n_out, m_out = rest[-4:]
    h = pl.program_id(1)
    seq_len, nb = q_ref.shape[0], q_ref.shape[1]
    R = seq_len * nb

    def rows(ref):
        return jnp.concatenate([ref[t] for t in range(seq_len)], axis=0)

    g = rows(g_ref)
    row = lax.broadcasted_iota(jnp.int32, g.shape, 0)
    bsum = _segment_cumsum(_log_sigmoid(g), row // nb, seq_len, row_step=nb)
    logi_col = _pick_lane(g, h)
    b_col = _pick_lane(bsum, h + n_heads)
    m_col = _pick_lane(rows(mrep_ref), h)

    r2 = lax.broadcasted_iota(jnp.int32, (R, R), 0)
    c2 = lax.broadcasted_iota(jnp.int32, (R, R), 1)
    eye = r2 == c2
    same = (r2 % nb) == (c2 % nb)
    b_row = _col_to_row(b_col, eye)
    logi_row = _col_to_row(logi_col, eye)
    dmat = jnp.where(same & (c2 <= r2), b_col - b_row + logi_row, -jnp.inf)
    inter = b_col + m_col
    m_t = jnp.maximum(inter, jnp.max(dmat, axis=1, keepdims=True))
    w_inter = jnp.exp(inter - m_t) * scale

    q = rows(q_ref)
    k = rows(k_ref)
    v = rows(v_ref)
    q32 = q.astype(F32)
    qk = lax.dot_general(q, k, (((1,), (1,)), ((), ())), preferred_element_type=F32) * scale
    s = qk * jnp.exp(dmat - m_t)
    num_s = jnp.dot(s.astype(BF16), v, preferred_element_type=F32)

    seq_of_row = lax.broadcasted_iota(jnp.int32, (R, 1), 0) % nb
    num_c = jnp.zeros(num_s.shape, F32)
    qn = jnp.zeros((R, 1), F32)
    for j in range(nb):
        mine = seq_of_row == j
        cq = lax.dot_general(q, c_ref[j, 0].astype(BF16), (((1,), (1,)), ((), ())),
                             preferred_element_type=F32)
        num_c = jnp.where(mine, cq, num_c)
        qn = jnp.where(mine, jnp.sum(q32 * n_ref[j, 0], axis=1, keepdims=True), qn)
    num = w_inter * num_c + num_s
    den = w_inter * qn + jnp.sum(s, axis=1, keepdims=True)
    hh = num * (1.0 / jnp.maximum(jnp.abs(den), jnp.exp(-m_t)))
    hm = _head_norm_gate(hh, rows(o_ref), mhg_ref[...])
    for t in range(seq_len):
        hm_ref[t] = hm[t * nb:(t + 1) * nb]

    pick_last = same & (c2 // nb == seq_len - 1)

    def last_of_seq(col):
        return jnp.sum(jnp.where(pick_last, _col_to_row(col, eye), 0.0), axis=1, keepdims=True)

    m_new = last_of_seq(m_t)
    w_c = jnp.exp(last_of_seq(inter) - m_new)
    w_s = jnp.exp(last_of_seq(b_col) - b_col + logi_col - m_new)
    vs = (w_s * v.astype(F32)).astype(BF16)
    kw = w_s * k.astype(F32)
    for j in range(nb):
        mine = seq_of_row == j
        upd = lax.dot_general(jnp.where(mine, vs, jnp.zeros_like(vs)), k,
                              (((0,), (0,)), ((), ())), preferred_element_type=F32)
        wc_j = w_c[j:j + 1, :]
        c_out[j, 0] = wc_j * c_ref[j, 0] + upd
        n_out[j, 0] = wc_j * n_ref[j, 0] + jnp.sum(jnp.where(mine, kw, 0.0), axis=0, keepdims=True)
    for t in range(seq_len):
        m_out[0, t] = m_new[t * nb:(t + 1) * nb]


def _mlstm_sample(qkvo_s, gates_s, mh_g, state_c, state_n, m_rep, c_prev, layer, n_heads, dqk, dv):
    seq_len, n_seq, _ = qkvo_s.shape
    nb = _tile(n_seq, SAMPLE_SEQS, BF16_SUBLANES)
    qw, vw = n_heads * dqk, n_heads * dv
    o_blk0 = (2 * qw + vw) // dv

    in_specs = [
        pl.BlockSpec((seq_len, nb, dqk), lambda i, h: (0, i, h)),
        pl.BlockSpec((seq_len, nb, dqk), lambda i, h: (0, i, n_heads + h)),
        pl.BlockSpec((seq_len, nb, dv), lambda i, h: (0, i, 2 * qw // dv + h)),
        pl.BlockSpec((seq_len, nb, dv), lambda i, h: (0, i, o_blk0 + h)),
        pl.BlockSpec((seq_len, nb, LANES), lambda i, h: (0, i, 0)),
        pl.BlockSpec((None, 1, dv), lambda i, h: (layer, 0, h)),
        pl.BlockSpec((None, nb, 1, dv, dqk), lambda i, h: (layer, i, h, 0, 0)),
        pl.BlockSpec((None, nb, 1, 1, dqk), lambda i, h: (layer, i, h, 0, 0)),
        pl.BlockSpec((seq_len, nb, n_heads), lambda i, h: (0, i, 0)),
    ]
    args = [qkvo_s, qkvo_s, qkvo_s, qkvo_s, gates_s, mh_g, state_c, state_n, m_rep]
    aliases = {}
    if c_prev is not None:
        in_specs.append(pl.BlockSpec(memory_space=pl.ANY))
        args.append(c_prev)
        aliases[len(args) - 1] = 1

    return pl.pallas_call(
        functools.partial(_mlstm_sample_kernel, n_heads=n_heads, scale=dqk ** -0.5),
        grid=(n_seq // nb, n_heads),
        in_specs=in_specs,
        out_specs=[
            pl.BlockSpec((seq_len, nb, dv), lambda i, h: (0, i, h)),
            pl.BlockSpec((None, nb, 1, dv, dqk), lambda i, h: (layer, i, h, 0, 0)),
            pl.BlockSpec((nb, 1, 1, dqk), lambda i, h: (i, h, 0, 0)),
            pl.BlockSpec((1, seq_len, nb, 1), lambda i, h: (h, 0, i, 0)),
        ],
        out_shape=[
            jax.ShapeDtypeStruct((seq_len, n_seq, vw), BF16),
            jax.ShapeDtypeStruct(state_c.shape, F32),
            jax.ShapeDtypeStruct(state_n.shape[1:], F32),
            jax.ShapeDtypeStruct((n_heads, seq_len, n_seq, 1), F32),
        ],
        input_output_aliases=aliases,
        compiler_params=_params(2),
        name="mlstm_sample",
    )(*args)


def _pool_conv_prompt_kernel(u_ref, uh_ref, cb_ref, cc_ref, cx_ref, cch_ref, cxh_ref,
                             maps_ref, scale_ref, cw_ref,
                             zp_ref, zs_ref, ps_ref, cs_ref, full_s, pf_s, *, windows):
    t = pl.program_id(1)
    last = pl.num_programs(1) - 1
    tr = u_ref.shape[0]
    gw = maps_ref.shape[1]
    n_buf = ps_ref.shape[1]
    taps = cw_ref.shape[0]
    has_past = t > 0

    full_s[0:POOL_HALO, :] = jnp.where(has_past, uh_ref[...], 0.0)
    full_s[POOL_HALO:, :] = u_ref[...]
    pf_s[0:CONV_HALO, :] = jnp.where(has_past, cch_ref[...] * cxh_ref[...], 0.0)
    pf_s[CONV_HALO:, :] = cc_ref[...] * cx_ref[...]
    pos = t * tr + lax.broadcasted_iota(jnp.int32, (tr, 1), 0)
    for g, w in enumerate(windows):
        cols = slice(g * gw, (g + 1) * gw)
        u = u_ref[:, cols]
        if w & (w - 1) == 0 and w <= POOL_HALO:
            win = full_s[:, cols]
            span = 1
            while span < w:
                win = win + pltpu.roll(win, span, axis=0)
                span *= 2
            acc = win[POOL_HALO:, :]
        else:
            acc = u
            for j in range(1, w):
                acc = acc + full_s[POOL_HALO - j:POOL_HALO - j + tr, cols]
        inv_cnt = 1.0 / jnp.minimum(pos + 1, w).astype(F32)
        zf = acc * inv_cnt - u
        zp = jnp.dot(zf.astype(BF16), maps_ref[g], preferred_element_type=F32) * scale_ref[:, cols]
        zp_ref[:, cols] = zp.astype(BF16)

        y = pf_s[CONV_HALO:CONV_HALO + tr, cols] * cw_ref[taps - 1:taps, cols]
        for j in range(taps - 1):
            back = taps - 1 - j
            y = y + pf_s[CONV_HALO - back:CONV_HALO - back + tr, cols] * cw_ref[j:j + 1, cols]
        zs_ref[:, cols] = (cb_ref[:, cols] * y).astype(BF16)

    @pl.when(t == last)
    def _():
        ps_ref[0] = full_s[POOL_HALO + tr - n_buf:POOL_HALO + tr, :]
        cs_ref[0] = pf_s[CONV_HALO + tr - (taps - 1):CONV_HALO + tr, :]


def _pool_conv_prompt(tail, maps, pscale, conv_w, layer, batch, seq, width, n_buf):
    tr = _tile(seq, 256, POOL_HALO)
    nt = seq // tr
    taps = conv_w.shape[1]
    n_groups, gw = maps.shape[1], maps.shape[2]
    hp = tr // POOL_HALO
    hc = tr // CONV_HALO

    def rows(b, t):
        return b * nt + t

    def main(col):
        return pl.BlockSpec((tr, width), lambda b, t: (rows(b, t), col))

    def halo(col, h, per_tile):
        return pl.BlockSpec((h, width), lambda b, t: (jnp.maximum(rows(b, t) * per_tile - 1, 0), col))

    return pl.pallas_call(
        functools.partial(_pool_conv_prompt_kernel, windows=POOL_WINDOWS),
        grid=(batch, nt),
        in_specs=[main(0), halo(0, POOL_HALO, hp), main(1), main(2), main(3),
                  halo(2, CONV_HALO, hc), halo(3, CONV_HALO, hc),
                  pl.BlockSpec((None, n_groups, gw, gw), lambda b, t: (layer, 0, 0, 0)),
                  pl.BlockSpec((None, 1, width), lambda b, t: (layer, 0, 0)),
                  pl.BlockSpec((None, taps, width), lambda b, t: (layer, 0, 0))],
        out_specs=[pl.BlockSpec((tr, width), lambda b, t: (rows(b, t), 0)),
                   pl.BlockSpec((tr, width), lambda b, t: (rows(b, t), 0)),
                   pl.BlockSpec((1, n_buf, width), lambda b, t: (b, 0, 0)),
                   pl.BlockSpec((1, taps - 1, width), lambda b, t: (b, 0, 0))],
        out_shape=[jax.ShapeDtypeStruct((tail.shape[0], width), BF16),
                   jax.ShapeDtypeStruct((tail.shape[0], width), BF16),
                   jax.ShapeDtypeStruct((batch, n_buf, width), F32),
                   jax.ShapeDtypeStruct((batch, taps - 1, width), F32)],
        scratch_shapes=[pltpu.VMEM((POOL_HALO + tr, width), F32),
                        pltpu.VMEM((CONV_HALO + tr, width), F32)],
        compiler_params=_params(2),
        name="pool_conv_prompt",
    )(tail, tail, tail, tail, tail, tail, tail, maps, pscale, conv_w)


def _pool_conv_sample_kernel(u_ref, cb_ref, cc_ref, cx_ref, ps_ref, cs_ref, maps_ref, scale_ref,
                             cw_ref, zp_ref, zs_ref, pn_ref, cn_ref, *, windows, start_pos):
    g = pl.program_id(0)
    seq_len = u_ref.shape[0]
    n_buf = ps_ref.shape[0]
    taps = cw_ref.shape[0]
    win = jnp.int32(windows[0])
    for i, w in enumerate(windows):
        win = jnp.where(g == i, jnp.int32(w), win)

    hist = [ps_ref[j] for j in range(n_buf)] + [u_ref[t] for t in range(seq_len)]
    for t in range(seq_len):
        cur = n_buf + t
        acc = hist[cur]
        for j in range(1, max(windows)):
            acc = acc + jnp.where(j < win, hist[cur - j], 0.0)
        cnt = jnp.minimum(start_pos + t + 1, win).astype(F32)
        zf = acc / cnt - hist[cur]
        zp = jnp.dot(zf.astype(BF16), maps_ref[0], preferred_element_type=F32) * scale_ref[...]
        zp_ref[t] = zp.astype(BF16)
    for j in range(n_buf):
        pn_ref[j] = hist[seq_len + j]

    prod = [cs_ref[j] for j in range(taps - 1)] + [cc_ref[t] * cx_ref[t] for t in range(seq_len)]
    for t in range(seq_len):
        y = prod[t] * cw_ref[0:1, :]
        for j in range(1, taps):
            y = y + prod[t + j] * cw_ref[j:j + 1, :]
        zs_ref[t] = (cb_ref[t] * y).astype(BF16)
    for j in range(taps - 1):
        cn_ref[j] = prod[seq_len + j]


def _pool_conv_sample(tail_s, pstate_t, cstate_t, maps, pscale, conv_w, layer):
    seq_len, n_seq, _ = tail_s.shape
    n_buf, width = pstate_t.shape[1], pstate_t.shape[3]
    taps = conv_w.shape[1]
    n_groups, gw = maps.shape[1], maps.shape[2]
    assert len(POOL_WINDOWS) == n_groups and n_buf >= max(POOL_WINDOWS) - 1

    def section(k):
        return pl.BlockSpec((seq_len, n_seq, gw), lambda g: (0, 0, k * n_groups + g))

    return pl.pallas_call(
        functools.partial(_pool_conv_sample_kernel, windows=POOL_WINDOWS, start_pos=PAST_LEN),
        grid=(n_groups,),
        in_specs=[section(0), section(1), section(2), section(3),
                  pl.BlockSpec((None, n_buf, n_seq, gw), lambda g: (layer, 0, 0, g)),
                  pl.BlockSpec((None, taps - 1, n_seq, gw), lambda g: (layer, 0, 0, g)),
                  pl.BlockSpec((None, 1, gw, gw), lambda g: (layer, g, 0, 0)),
                  pl.BlockSpec((None, 1, gw), lambda g: (layer, 0, g)),
                  pl.BlockSpec((None, taps, gw), lambda g: (layer, 0, g))],
        out_specs=[pl.BlockSpec((seq_len, n_seq, gw), lambda g: (0, 0, g)),
                   pl.BlockSpec((seq_len, n_seq, gw), lambda g: (0, 0, g)),
                   pl.BlockSpec((n_buf, n_seq, gw), lambda g: (0, 0, g)),
                   pl.BlockSpec((taps - 1, n_seq, gw), lambda g: (0, 0, g))],
        out_shape=[jax.ShapeDtypeStruct((seq_len, n_seq, width), BF16),
                   jax.ShapeDtypeStruct((seq_len, n_seq, width), BF16),
                   jax.ShapeDtypeStruct(pstate_t.shape[1:], F32),
                   jax.ShapeDtypeStruct(cstate_t.shape[1:], F32)],
        compiler_params=_params(1),
        name="pool_conv_sample",
    )(tail_s, tail_s, tail_s, tail_s, pstate_t, cstate_t, maps, pscale, conv_w)


def kernel(x_prompt, x_sample, state_C, state_n, state_m, state_pool, state_conv, w_in, b_in, mh_g, pool_maps, pool_scale, conv_w, w_bm, w_bp, w_bs, w_o, ln1_g, ln1_b, w_gate, w_up, w_down, ln2_g, ln2_b):
    depth = w_in.shape[0]
    batch, seq, d = x_prompt.shape
    n_seq, seq_len, _ = x_sample.shape
    n_heads, dv, dqk = state_C.shape[2], state_C.shape[3], state_C.shape[4]
    n_buf, pw = state_pool.shape[2], state_pool.shape[3]
    cw = conv_w.shape[2]
    dff = w_gate.shape[2]
    assert pw == cw and dv == 2 * dqk and 2 * n_heads <= LANES
    alpha = (2 * depth) ** 0.25
    tp, ts = batch * seq, n_seq * seq_len
    t_all = tp + ts

    off_i = 2 * n_heads * dqk + 2 * n_heads * dv
    off_p = off_i + 2 * n_heads
    gate_col0 = pw + 3 * cw
    k_blocks = 2 if dff % (2 * LANES) == 0 else 1

    w_in_t = jnp.swapaxes(w_in, 1, 2)
    w_if = jnp.pad(w_in[:, :, off_i:off_p], ((0, 0), (0, 0), (0, LANES - (off_p - off_i)))).astype(BF16)
    b3 = b_in[:, None, :]
    b_qkvo, b_tail = b3[:, :, :off_i], b3[:, :, off_p:]
    b_if = jnp.pad(b3[:, :, off_i:off_p], ((0, 0), (0, 0), (0, LANES - (off_p - off_i))))
    wbm, wbp, wbs = (_to_bf16(w) for w in (w_bm, w_bp, w_bs))
    maps = pool_maps.astype(BF16)
    mhg3, pscale3 = mh_g[:, None, :], pool_scale[:, None, :]
    ln1g, ln1b, ln2g, ln2b = (a[:, None, :] for a in (ln1_g, ln1_b, ln2_g, ln2_b))
    state_n5 = state_n[:, :, :, None, :]
    pool_t = state_pool.transpose(0, 2, 1, 3)
    conv_t = state_conv.transpose(0, 2, 1, 3)

    x, xb = _stack_rows(x_prompt.reshape(tp, d), x_sample.transpose(1, 0, 2))

    outs = [[] for _ in range(9)]
    c_s = None
    for l in range(depth):
        qkvo = _matmul_nt(xb, w_in_t, l, 0, off_i, b_qkvo, out_dtype=BF16, name="in_qkvo")
        gates = _matmul(xb, w_if, l, bias=b_if, name="in_if")
        tail = _matmul_nt(xb, w_in_t, l, off_p, w_in.shape[2] - off_p, b_tail, name="in_tail")

        hm, c_p, n_p, m_p = _mlstm_prompt(qkvo, gates, mhg3, l, batch, seq, t_all, n_heads, dqk, dv)
        m_rep = jnp.broadcast_to(state_m[l][None], (seq_len, n_seq, n_heads))
        hm_s, c_s, n_s, m_s = _mlstm_sample(
            qkvo[tp:].reshape(seq_len, n_seq, -1), gates[tp:].reshape(seq_len, n_seq, -1), mhg3,
            state_C, state_n5, m_rep, c_s, l, n_heads, dqk, dv)
        hm = lax.dynamic_update_slice(hm, hm_s.reshape(ts, -1), (tp, 0))

        zp, zs, ps_p, cs_p = _pool_conv_prompt(tail, maps, pscale3, conv_w, l, batch, seq, pw, n_buf)
        tail_s = tail[tp:, :pw + 3 * cw].reshape(seq_len, n_seq, -1)
        zp_s, zs_s, ps_s, cs_s = _pool_conv_sample(tail_s, pool_t, conv_t, maps, pscale3, conv_w, l)
        zp = lax.dynamic_update_slice(zp, zp_s.reshape(ts, pw), (tp, 0))
        zs = lax.dynamic_update_slice(zs, zs_s.reshape(ts, cw), (tp, 0))

        mixed = _merge(hm, zp, zs, wbm, wbp, wbs, l, tail, gate_col0)
        r1 = _matmul(mixed, w_o, l, res=x, res_scale=alpha, name="out_proj")
        x1, x1b = _layernorm(r1, ln1g, ln1b, l)

        ff = _ffn_up(x1b, w_gate, w_up, l, tn=FF_TILE)
        r2 = _matmul(ff, w_down, l, res=x1, res_scale=alpha, tn=FF_TILE, k_blocks=k_blocks,
                     name="ffn_down")
        for kb in range(1, k_blocks):
            r2 = _matmul(ff, w_down, l, res=r2, tn=FF_TILE, k_blocks=k_blocks, k_index=kb,
                         name="ffn_down")
        if l + 1 < depth:
            x, xb = _layernorm(r2, ln2g, ln2b, l)
        else:
            y_p, = _layernorm(r2, ln2g, ln2b, l, 0, tp, with_bf16=False)
            y_s, = _layernorm(r2, ln2g, ln2b, l, tp, ts, with_bf16=False, time_major_seqs=n_seq)

        new = (c_p, n_p, m_p[:, :, 0], ps_p, cs_p,
               n_s[:, :, 0, :], m_s[:, seq_len - 1, :, 0].T,
               ps_s.transpose(1, 0, 2), cs_s.transpose(1, 0, 2))
        for acc, val in zip(outs, new):
            acc.append(val)

    st = [jnp.stack(o) for o in outs]
    return (y_p.reshape(batch, seq, d), y_s.transpose(1, 0, 2),
            st[0], st[1], st[2], st[3], st[4], c_s, st[5], st[6], st[7], st[8])
```

```python
import functools

import jax
import jax.numpy as jnp
from jax import lax
from jax.experimental import pallas as pl
from jax.experimental.pallas import tpu as pltpu

F32 = jnp.float32
BF16 = jnp.bfloat16

POOL_WINDOWS = (2, 4, 8, 16)
PAST_LEN = 16384
LN_EPS = 1e-5
PROMPT_CHUNK = 256

LANES = 128
BF16_SUBLANES = 16
VMEM_LIMIT = 56 * 1024 * 1024
IN_PROJ_VMEM_LIMIT = 60 * 1024 * 1024
POOL_HALO = 16
CONV_HALO = 8
HEAD_GROUP = 2
SAMPLE_SEQS = 16
FF_TILE = 256


def _params(n_axes):
    return pltpu.CompilerParams(dimension_semantics=("arbitrary",) * n_axes,
                                vmem_limit_bytes=VMEM_LIMIT)


def _tile(dim, target, mult):
    best = None
    for d in range(mult, min(dim, target) + 1, mult):
        if dim % d == 0:
            best = d
    return dim if best is None else best


def _log_sigmoid(x):
    return jnp.minimum(x, 0.0) - jnp.log(1.0 + jnp.exp(-jnp.abs(x)))


def _cast_kernel(x_ref, o_ref):
    o_ref[...] = x_ref[...].astype(o_ref.dtype)


def _to_bf16(w):
    layers, rows, cols = w.shape
    tc = _tile(cols, 4096, LANES)
    tr = _tile(rows, max(BF16_SUBLANES, 1024 * 1024 // tc), BF16_SUBLANES)
    spec = pl.BlockSpec((None, tr, tc), lambda l, i, j: (l, i, j))
    return pl.pallas_call(
        _cast_kernel,
        grid=(layers, rows // tr, cols // tc),
        in_specs=[spec],
        out_specs=spec,
        out_shape=jax.ShapeDtypeStruct((layers, rows, cols), BF16),
        compiler_params=_params(3),
        name="to_bf16",
    )(w)


def _stack_kernel(xp_ref, xs_ref, of_ref, ob_ref, *, prompt_tiles):
    x = jnp.where(pl.program_id(0) < prompt_tiles, xp_ref[...], xs_ref[...])
    of_ref[...] = x
    ob_ref[...] = x.astype(BF16)


def _stack_rows(xp, xs):
    tp, d = xp.shape
    seq_len, n_seq, _ = xs.shape
    tr = n_seq
    assert tp % tr == 0 and tr % BF16_SUBLANES == 0
    n_p = tp // tr
    out = pl.BlockSpec((tr, d), lambda i: (i, 0))
    return pl.pallas_call(
        functools.partial(_stack_kernel, prompt_tiles=n_p),
        grid=(n_p + seq_len,),
        in_specs=[pl.BlockSpec((tr, d), lambda i: (jnp.minimum(i, n_p - 1), 0)),
                  pl.BlockSpec((None, n_seq, d), lambda i: (jnp.maximum(i - n_p, 0), 0, 0))],
        out_specs=[out, out],
        out_shape=[jax.ShapeDtypeStruct((tp + n_seq * seq_len, d), F32),
                   jax.ShapeDtypeStruct((tp + n_seq * seq_len, d), BF16)],
        compiler_params=_params(1),
        name="stack_rows",
    )(xp, xs)


def _mm_kernel(*refs, has_bias, has_res, res_scale):
    x_ref, w_ref = refs[0], refs[1]
    o_ref = refs[-1]
    acc = jnp.dot(x_ref[...], w_ref[...].astype(BF16), preferred_element_type=F32)
    pos = 2
    if has_bias:
        acc = acc + refs[pos][...]
        pos += 1
    if has_res:
        acc = acc + res_scale * refs[pos][...]
    o_ref[...] = acc.astype(o_ref.dtype)


def _matmul(x, w, layer, bias=None, res=None, res_scale=1.0, out_dtype=F32, tm=1088, tn=512,
            k_blocks=1, k_index=0, name="matmul"):
    t = x.shape[0]
    n = w.shape[2]
    k = x.shape[1] // k_blocks
    tm = _tile(t, tm, BF16_SUBLANES)
    tn = _tile(n, tn, LANES)
    in_specs = [pl.BlockSpec((tm, k), lambda i, j: (i, k_index)),
                pl.BlockSpec((None, k, tn), lambda i, j: (layer, k_index, j))]
    args = [x, w]
    if bias is not None:
        in_specs.append(pl.BlockSpec((None, 1, tn), lambda i, j: (layer, 0, j)))
        args.append(bias)
    if res is not None:
        in_specs.append(pl.BlockSpec((tm, tn), lambda i, j: (i, j)))
        args.append(res)
    return pl.pallas_call(
        functools.partial(_mm_kernel, has_bias=bias is not None, has_res=res is not None,
                          res_scale=res_scale),
        grid=(t // tm, n // tn),
        in_specs=in_specs,
        out_specs=pl.BlockSpec((tm, tn), lambda i, j: (i, j)),
        out_shape=jax.ShapeDtypeStruct((t, n), out_dtype),
        compiler_params=_params(2),
        name=name,
    )(*args)


def _in_proj_kernel(x_ref, a_ref, sh_ref, ba_ref, bb_ref, oa_ref, ob_ref, *, tiles_a, shift):
    j = pl.program_id(1)
    dims = (((1,), (1,)), ((), ()))

    @pl.when(j < tiles_a)
    def _():
        acc = lax.dot_general(x_ref[...], a_ref[...].astype(BF16), dims, preferred_element_type=F32)
        oa_ref[...] = (acc + ba_ref[...]).astype(oa_ref.dtype)

    @pl.when(j >= tiles_a)
    def _():
        w = a_ref[...].astype(BF16)
        w = jnp.concatenate([w[shift:], sh_ref[...].astype(BF16)], axis=0)
        acc = lax.dot_general(x_ref[...], w, dims, preferred_element_type=F32)
        ob_ref[...] = (acc + bb_ref[...]).astype(ob_ref.dtype)


def _in_proj(x, wt, layer, n_a, row_b, n_b, bias_a, bias_b, tm=2176, tn=512):
    t, k = x.shape
    tm = _tile(t, tm, BF16_SUBLANES)
    tn = _tile(n_a, tn, LANES)
    shift = row_b - n_a
    assert 0 < shift < tn and shift % BF16_SUBLANES == 0 and tn % shift == 0 and n_b % tn == 0
    ta, tb = n_a // tn, n_b // tn

    def in_b(j):
        return jnp.maximum(j, ta) - ta

    return pl.pallas_call(
        functools.partial(_in_proj_kernel, tiles_a=ta, shift=shift),
        grid=(t // tm, ta + tb),
        in_specs=[pl.BlockSpec((tm, k), lambda i, j: (i, 0), pipeline_mode=pl.Buffered(1)),
                  pl.BlockSpec((None, tn, k), lambda i, j: (layer, j, 0)),
                  pl.BlockSpec((None, shift, k),
                               lambda i, j: (layer, (n_a + (in_b(j) + 1) * tn) // shift, 0)),
                  pl.BlockSpec((None, 1, tn), lambda i, j: (layer, 0, jnp.minimum(j, ta - 1))),
                  pl.BlockSpec((None, 1, tn), lambda i, j: (layer, 0, in_b(j)))],
        out_specs=[pl.BlockSpec((tm, tn), lambda i, j: (i, jnp.minimum(j, ta - 1))),
                   pl.BlockSpec((tm, tn), lambda i, j: (i, in_b(j)))],
        out_shape=[jax.ShapeDtypeStruct((t, n_a), BF16), jax.ShapeDtypeStruct((t, n_b), F32)],
        compiler_params=pltpu.CompilerParams(dimension_semantics=("arbitrary",) * 2,
                                             vmem_limit_bytes=IN_PROJ_VMEM_LIMIT),
        name="in_proj",
    )(x, wt, wt, bias_a, bias_b)


def _ln_kernel(r_ref, g_ref, b_ref, of_ref, *maybe_bf16_ref):
    r = r_ref[...]
    mu = jnp.mean(r, axis=1, keepdims=True)
    d = r - mu
    var = jnp.mean(d * d, axis=1, keepdims=True)
    y = d * lax.rsqrt(var + LN_EPS) * g_ref[...] + b_ref[...]
    of_ref[...] = y
    for ob_ref in maybe_bf16_ref:
        ob_ref[...] = y.astype(BF16)


def _layernorm(r, g, b, layer, row0=0, n_rows=None, with_bf16=True, time_major_seqs=None):
    d = r.shape[1]
    n_rows = r.shape[0] - row0 if n_rows is None else n_rows
    tr = BF16_SUBLANES
    for cand in range(BF16_SUBLANES, 544 + 1, BF16_SUBLANES):
        if row0 % cand == 0 and n_rows % cand == 0:
            tr = cand
    out_row = pl.BlockSpec((tr, d), lambda i: (i, 0))
    out_shape = [jax.ShapeDtypeStruct((n_rows, d), F32)]
    if time_major_seqs is not None:
        tr = time_major_seqs
        assert row0 % tr == 0 and n_rows % tr == 0 and not with_bf16
        out_row = pl.BlockSpec((None, tr, d), lambda i: (i, 0, 0))
        out_shape = [jax.ShapeDtypeStruct((n_rows // tr, tr, d), F32)]
    blk0 = row0 // tr
    vec = pl.BlockSpec((None, 1, d), lambda i: (layer, 0, 0))
    out_specs = [out_row]
    if with_bf16:
        out_specs.append(out_row)
        out_shape.append(jax.ShapeDtypeStruct((n_rows, d), BF16))
    return pl.pallas_call(
        _ln_kernel,
        grid=(n_rows // tr,),
        in_specs=[pl.BlockSpec((tr, d), lambda i: (blk0 + i, 0)), vec, vec],
        out_specs=out_specs,
        out_shape=out_shape,
        compiler_params=_params(1),
        name="layernorm",
    )(r, g, b)


def _ffn_up_kernel(x_ref, wg_ref, wu_ref, o_ref):
    x = x_ref[...]
    a = jnp.dot(x, wg_ref[...].astype(BF16), preferred_element_type=F32)
    u = jnp.dot(x, wu_ref[...].astype(BF16), preferred_element_type=F32)
    o_ref[...] = (a * jax.nn.sigmoid(a) * u).astype(o_ref.dtype)


def _ffn_up(x, wg, wu, layer, tm=2176, tn=512):
    t, k = x.shape
    n = wg.shape[2]
    tm = _tile(t, tm, BF16_SUBLANES)
    tn = _tile(n, tn, LANES)
    wspec = pl.BlockSpec((None, k, tn), lambda i, j: (layer, 0, j))
    return pl.pallas_call(
        _ffn_up_kernel,
        grid=(t // tm, n // tn),
        in_specs=[pl.BlockSpec((tm, k), lambda i, j: (i, 0), pipeline_mode=pl.Buffered(1)),
                  wspec, wspec],
        out_specs=pl.BlockSpec((tm, tn), lambda i, j: (i, j)),
        out_shape=jax.ShapeDtypeStruct((t, n), BF16),
        compiler_params=_params(2),
        name="ffn_up",
    )(x, wg, wu)


def _merge_kernel(hm_ref, zp_ref, zs_ref, wm_ref, wp_ref, ws_ref, gm_ref, gp_ref, gs_ref, o_ref):
    ym = jnp.dot(hm_ref[...], wm_ref[...], preferred_element_type=F32)
    yp = jnp.dot(zp_ref[...], wp_ref[...], preferred_element_type=F32)
    ys = jnp.dot(zs_ref[...], ws_ref[...], preferred_element_type=F32)
    mixed = (jax.nn.sigmoid(gm_ref[...]) * ym + jax.nn.sigmoid(gp_ref[...]) * yp
             + jax.nn.sigmoid(gs_ref[...]) * ys)
    o_ref[...] = mixed.astype(o_ref.dtype)


def _merge(hm, zp, zs, wm, wp, ws, layer, tail, gate_col0, tm=544, tn=512):
    t = hm.shape[0]
    d = wm.shape[2]
    tm = _tile(t, tm, BF16_SUBLANES)
    tn = _tile(d, tn, LANES)
    g0 = gate_col0 // tn
    gd = d // tn

    def xspec(a):
        return pl.BlockSpec((tm, a.shape[1]), lambda i, j: (i, 0))

    def wspec(a):
        return pl.BlockSpec((None, a.shape[1], tn), lambda i, j: (layer, 0, j))

    def gspec(branch):
        return pl.BlockSpec((tm, tn), lambda i, j: (i, g0 + branch * gd + j))

    return pl.pallas_call(
        _merge_kernel,
        grid=(t // tm, d // tn),
        in_specs=[xspec(hm), xspec(zp), xspec(zs), wspec(wm), wspec(wp), wspec(ws),
                  gspec(0), gspec(1), gspec(2)],
        out_specs=pl.BlockSpec((tm, tn), lambda i, j: (i, j)),
        out_shape=jax.ShapeDtypeStruct((t, d), BF16),
        compiler_params=_params(2),
        name="merge",
    )(hm, zp, zs, wm, wp, ws, tail, tail, tail)


def _pick_lane(x, lane_index):
    lane = lax.broadcasted_iota(jnp.int32, x.shape, 1)
    return jnp.sum(jnp.where(lane == lane_index, x, 0.0), axis=1, keepdims=True)


def _col_to_row(col, eye):
    return jnp.sum(jnp.where(eye, col, 0.0), axis=0, keepdims=True)


def _segment_cumsum(x, seg_pos, seg_len, row_step=1):
    s = 1
    while s < seg_len:
        x = x + jnp.where(seg_pos >= s, pltpu.roll(x, s * row_step, axis=0), 0.0)
        s *= 2
    return x


def _head_norm_gate(hh, o_pre, mh_g):
    mu = jnp.mean(hh, axis=1, keepdims=True)
    d = hh - mu
    var = jnp.mean(d * d, axis=1, keepdims=True)
    return (jax.nn.sigmoid(o_pre.astype(F32)) * (d * lax.rsqrt(var + LN_EPS) * mh_g)).astype(BF16)


def _mlstm_prompt_kernel(q_ref, k_ref, v_ref, o_ref, g_ref, mhg_ref,
                         hm_ref, c_out, n_out, m_out,
                         ct_s, n_s, m_s, *, n_heads, dqk, dv, scale):
    c = pl.program_id(1)
    last = pl.num_programs(1) - 1
    L = q_ref.shape[0]

    @pl.when(c == 0)
    def _():
        ct_s[...] = jnp.zeros_like(ct_s)
        n_s[...] = jnp.zeros_like(n_s)
        m_s[...] = jnp.zeros_like(m_s)

    g = g_ref[...]
    row = lax.broadcasted_iota(jnp.int32, g.shape, 0)
    lane = lax.broadcasted_iota(jnp.int32, g.shape, 1)
    bsum = _segment_cumsum(_log_sigmoid(g), row, L)
    gb = jnp.where(lane < n_heads, g, bsum)
    gb_t = gb.T
    r2 = lax.broadcasted_iota(jnp.int32, (L, L), 0)
    c2 = lax.broadcasted_iota(jnp.int32, (L, L), 1)
    causal = r2 >= c2

    for h0 in range(0, n_heads, HEAD_GROUP):
        heads = range(h0, min(h0 + HEAD_GROUP, n_heads))
        st = {}
        for h in heads:
            logi_col = gb[:, h:h + 1]
            b_col = gb[:, n_heads + h:n_heads + h + 1]
            logi_row = gb_t[h:h + 1, :]
            b_row = gb_t[n_heads + h:n_heads + h + 1, :]
            dmat = jnp.where(causal, b_col - b_row + logi_row, -jnp.inf)
            inter = b_col + m_s[h:h + 1, :]
            m_t = jnp.maximum(inter, jnp.max(dmat, axis=1, keepdims=True))
            st[h] = dict(logi_col=logi_col, b_col=b_col, inter=inter, m_t=m_t,
                         w_inter=jnp.exp(inter - m_t) * scale,
                         decay=jnp.exp(dmat - m_t))
        for h in heads:
            q = q_ref[:, h * dqk:(h + 1) * dqk]
            k = k_ref[:, h * dqk:(h + 1) * dqk]
            ct = ct_s[h]
            qk = lax.dot_general(q, k, (((1,), (1,)), ((), ())), preferred_element_type=F32)
            st[h]["s"] = qk * scale * st[h]["decay"]
            st[h]["qc"] = jnp.dot(q, ct.astype(BF16), preferred_element_type=F32)
            st[h]["qn"] = jnp.sum(q.astype(F32) * n_s[h:h + 1, :], axis=1, keepdims=True)
        for h in heads:
            e = st[h]
            v = v_ref[:, h * dv:(h + 1) * dv]
            num = e["w_inter"] * e["qc"] + jnp.dot(e["s"].astype(BF16), v,
                                                   preferred_element_type=F32)
            den = e["w_inter"] * e["qn"] + jnp.sum(e["s"], axis=1, keepdims=True)
            hh = num * (1.0 / jnp.maximum(jnp.abs(den), jnp.exp(-e["m_t"])))
            hm_ref[:, h * dv:(h + 1) * dv] = _head_norm_gate(
                hh, o_ref[:, h * dv:(h + 1) * dv], mhg_ref[:, h * dv:(h + 1) * dv])
        for h in heads:
            e = st[h]
            k = k_ref[:, h * dqk:(h + 1) * dqk]
            v = v_ref[:, h * dv:(h + 1) * dv]
            m_new = e["m_t"][L - 1:L, :]
            w_c = jnp.exp(e["inter"][L - 1:L, :] - m_new)
            w_s = jnp.exp(e["b_col"][L - 1:L, :] - e["b_col"] + e["logi_col"] - m_new)
            vs = (w_s * v.astype(F32)).astype(BF16)
            ct_s[h] = w_c * ct_s[h] + lax.dot_general(k, vs, (((0,), (0,)), ((), ())),
                                                      preferred_element_type=F32)
            n_s[h:h + 1, :] = (w_c * n_s[h:h + 1, :]
                               + jnp.sum(w_s * k.astype(F32), axis=0, keepdims=True))
            m_s[h:h + 1, :] = m_new

    @pl.when(c == last)
    def _():
        for h in range(n_heads):
            c_out[0, h] = ct_s[h].T
        n_out[0] = n_s[...]
        m_out[0] = m_s[...]


def _mlstm_prompt(qkvo, gates, mh_g, layer, batch, seq, n_rows, n_heads, dqk, dv):
    L = PROMPT_CHUNK if seq % PROMPT_CHUNK == 0 else seq
    nc = seq // L
    qw, vw = n_heads * dqk, n_heads * dv
    assert qw % LANES == 0 and 2 * qw % vw == 0

    def rows(b, c):
        return b * nc + c

    return pl.pallas_call(
        functools.partial(_mlstm_prompt_kernel, n_heads=n_heads, dqk=dqk, dv=dv,
                          scale=dqk ** -0.5),
        grid=(batch, nc),
        in_specs=[
            pl.BlockSpec((L, qw), lambda b, c: (rows(b, c), 0)),
            pl.BlockSpec((L, qw), lambda b, c: (rows(b, c), 1)),
            pl.BlockSpec((L, vw), lambda b, c: (rows(b, c), 2 * qw // vw)),
            pl.BlockSpec((L, vw), lambda b, c: (rows(b, c), 2 * qw // vw + 1)),
            pl.BlockSpec((L, LANES), lambda b, c: (rows(b, c), 0)),
            pl.BlockSpec((None, 1, vw), lambda b, c: (layer, 0, 0)),
        ],
        out_specs=[
            pl.BlockSpec((L, vw), lambda b, c: (rows(b, c), 0)),
            pl.BlockSpec((1, n_heads, dv, dqk), lambda b, c: (b, 0, 0, 0)),
            pl.BlockSpec((1, n_heads, dqk), lambda b, c: (b, 0, 0)),
            pl.BlockSpec((1, n_heads, 1), lambda b, c: (b, 0, 0)),
        ],
        out_shape=[
            jax.ShapeDtypeStruct((n_rows, vw), BF16),
            jax.ShapeDtypeStruct((batch, n_heads, dv, dqk), F32),
            jax.ShapeDtypeStruct((batch, n_heads, dqk), F32),
            jax.ShapeDtypeStruct((batch, n_heads, 1), F32),
        ],
        scratch_shapes=[pltpu.VMEM((n_heads, dqk, dv), F32), pltpu.VMEM((n_heads, dqk), F32),
                        pltpu.VMEM((n_heads, 1), F32)],
        compiler_params=_params(2),
        name="mlstm_prompt",
    )(qkvo, qkvo, qkvo, qkvo, gates, mh_g)


def _mlstm_sample_kernel(q_ref, k_ref, v_ref, o_ref, g_ref, mhg_ref, c_ref, n_ref, mrep_ref,
                         *rest, n_heads, scale):
    hm_ref, c_out, n_out, m_out = rest[-4:]
    h = pl.program_id(1)
    seq_len, nb = q_ref.shape[0], q_ref.shape[1]
    R = seq_len * nb

    def rows(ref):
        return jnp.concatenate([ref[t] for t in range(seq_len)], axis=0)

    g = rows(g_ref)
    row = lax.broadcasted_iota(jnp.int32, g.shape, 0)
    bsum = _segment_cumsum(_log_sigmoid(g), row // nb, seq_len, row_step=nb)
    logi_col = _pick_lane(g, h)
    b_col = _pick_lane(bsum, h + n_heads)
    m_col = _pick_lane(rows(mrep_ref), h)

    r2 = lax.broadcasted_iota(jnp.int32, (R, R), 0)
    c2 = lax.broadcasted_iota(jnp.int32, (R, R), 1)
    eye = r2 == c2
    same = (r2 % nb) == (c2 % nb)
    b_row = _col_to_row(b_col, eye)
    logi_row = _col_to_row(logi_col, eye)
    dmat = jnp.where(same & (c2 <= r2), b_col - b_row + logi_row, -jnp.inf)
    inter = b_col + m_col
    m_t = jnp.maximum(inter, jnp.max(dmat, axis=1, keepdims=True))
    w_inter = jnp.exp(inter - m_t) * scale

    q = rows(q_ref)
    k = rows(k_ref)
    v = rows(v_ref)
    q32 = q.astype(F32)
    qk = lax.dot_general(q, k, (((1,), (1,)), ((), ())), preferred_element_type=F32) * scale
    s = qk * jnp.exp(dmat - m_t)
    num_s = jnp.dot(s.astype(BF16), v, preferred_element_type=F32)

    seq_of_row = lax.broadcasted_iota(jnp.int32, (R, 1), 0) % nb
    num_c = jnp.zeros(num_s.shape, F32)
    qn = jnp.zeros((R, 1), F32)
    for j in range(nb):
        mine = seq_of_row == j
        cq = lax.dot_general(q, c_ref[j, 0].astype(BF16), (((1,), (1,)), ((), ())),
                             preferred_element_type=F32)
        num_c = jnp.where(mine, cq, num_c)
        qn = jnp.where(mine, jnp.sum(q32 * n_ref[j, 0], axis=1, keepdims=True), qn)
    num = w_inter * num_c + num_s
    den = w_inter * qn + jnp.sum(s, axis=1, keepdims=True)
    hh = num * (1.0 / jnp.maximum(jnp.abs(den), jnp.exp(-m_t)))
    hm = _head_norm_gate(hh, rows(o_ref), mhg_ref[...])
    for t in range(seq_len):
        hm_ref[t] = hm[t * nb:(t + 1) * nb]

    pick_last = same & (c2 // nb == seq_len - 1)

    def last_of_seq(col):
        return jnp.sum(jnp.where(pick_last, _col_to_row(col, eye), 0.0), axis=1, keepdims=True)

    m_new = last_of_seq(m_t)
    w_c = jnp.exp(last_of_seq(inter) - m_new)
    w_s = jnp.exp(last_of_seq(b_col) - b_col + logi_col - m_new)
    vs = (w_s * v.astype(F32)).astype(BF16)
    kw = w_s * k.astype(F32)
    for j in range(nb):
        mine = seq_of_row == j
        upd = lax.dot_general(jnp.where(mine, vs, jnp.zeros_like(vs)), k,
                              (((0,), (0,)), ((), ())), preferred_element_type=F32)
        wc_j = w_c[j:j + 1, :]
        c_out[j, 0] = wc_j * c_ref[j, 0] + upd
        n_out[j, 0] = wc_j * n_ref[j, 0] + jnp.sum(jnp.where(mine, kw, 0.0), axis=0, keepdims=True)
    for t in range(seq_len):
        m_out[0, t] = m_new[t * nb:(t + 1) * nb]


def _mlstm_sample(qkvo_s, gates_s, mh_g, state_c, state_n, m_rep, c_prev, layer, n_heads, dqk, dv):
    seq_len, n_seq, _ = qkvo_s.shape
    nb = _tile(n_seq, SAMPLE_SEQS, BF16_SUBLANES)
    qw, vw = n_heads * dqk, n_heads * dv
    o_blk0 = (2 * qw + vw) // dv

    in_specs = [
        pl.BlockSpec((seq_len, nb, dqk), lambda i, h: (0, i, h)),
        pl.BlockSpec((seq_len, nb, dqk), lambda i, h: (0, i, n_heads + h)),
        pl.BlockSpec((seq_len, nb, dv), lambda i, h: (0, i, 2 * qw // dv + h)),
        pl.BlockSpec((seq_len, nb, dv), lambda i, h: (0, i, o_blk0 + h)),
        pl.BlockSpec((seq_len, nb, LANES), lambda i, h: (0, i, 0)),
        pl.BlockSpec((None, 1, dv), lambda i, h: (layer, 0, h)),
        pl.BlockSpec((None, nb, 1, dv, dqk), lambda i, h: (layer, i, h, 0, 0)),
        pl.BlockSpec((None, nb, 1, 1, dqk), lambda i, h: (layer, i, h, 0, 0)),
        pl.BlockSpec((seq_len, nb, n_heads), lambda i, h: (0, i, 0)),
    ]
    args = [qkvo_s, qkvo_s, qkvo_s, qkvo_s, gates_s, mh_g, state_c, state_n, m_rep]
    aliases = {}
    if c_prev is not None:
        in_specs.append(pl.BlockSpec(memory_space=pl.ANY))
        args.append(c_prev)
        aliases[len(args) - 1] = 1

    return pl.pallas_call(
        functools.partial(_mlstm_sample_kernel, n_heads=n_heads, scale=dqk ** -0.5),
        grid=(n_seq // nb, n_heads),
        in_specs=in_specs,
        out_specs=[
            pl.BlockSpec((seq_len, nb, dv), lambda i, h: (0, i, h)),
            pl.BlockSpec((None, nb, 1, dv, dqk), lambda i, h: (layer, i, h, 0, 0)),
            pl.BlockSpec((nb, 1, 1, dqk), lambda i, h: (i, h, 0, 0)),
            pl.BlockSpec((1, seq_len, nb, 1), lambda i, h: (h, 0, i, 0)),
        ],
        out_shape=[
            jax.ShapeDtypeStruct((seq_len, n_seq, vw), BF16),
            jax.ShapeDtypeStruct(state_c.shape, F32),
            jax.ShapeDtypeStruct(state_n.shape[1:], F32),
            jax.ShapeDtypeStruct((n_heads, seq_len, n_seq, 1), F32),
        ],
        input_output_aliases=aliases,
        compiler_params=_params(2),
        name="mlstm_sample",
    )(*args)


def _pool_conv_prompt_kernel(u_ref, uh_ref, cb_ref, cc_ref, cx_ref, cch_ref, cxh_ref,
                             maps_ref, scale_ref, cw_ref,
                             zp_ref, zs_ref, ps_ref, cs_ref, full_s, pf_s, *, windows):
    t = pl.program_id(1)
    last = pl.num_programs(1) - 1
    tr = u_ref.shape[0]
    gw = maps_ref.shape[1]
    n_buf = ps_ref.shape[1]
    taps = cw_ref.shape[0]
    has_past = t > 0

    full_s[0:POOL_HALO, :] = jnp.where(has_past, uh_ref[...], 0.0)
    full_s[POOL_HALO:, :] = u_ref[...]
    pf_s[0:CONV_HALO, :] = jnp.where(has_past, cch_ref[...] * cxh_ref[...], 0.0)
    pf_s[CONV_HALO:, :] = cc_ref[...] * cx_ref[...]
    pos = t * tr + lax.broadcasted_iota(jnp.int32, (tr, 1), 0)
    for g, w in enumerate(windows):
        cols = slice(g * gw, (g + 1) * gw)
        u = u_ref[:, cols]
        if w & (w - 1) == 0 and w <= POOL_HALO:
            win = full_s[:, cols]
            span = 1
            while span < w:
                win = win + pltpu.roll(win, span, axis=0)
                span *= 2
            acc = win[POOL_HALO:, :]
        else:
            acc = u
            for j in range(1, w):
                acc = acc + full_s[POOL_HALO - j:POOL_HALO - j + tr, cols]
        inv_cnt = 1.0 / jnp.minimum(pos + 1, w).astype(F32)
        zf = acc * inv_cnt - u
        zp = jnp.dot(zf.astype(BF16), maps_ref[g], preferred_element_type=F32) * scale_ref[:, cols]
        zp_ref[:, cols] = zp.astype(BF16)

        y = pf_s[CONV_HALO:CONV_HALO + tr, cols] * cw_ref[taps - 1:taps, cols]
        for j in range(taps - 1):
            back = taps - 1 - j
            y = y + pf_s[CONV_HALO - back:CONV_HALO - back + tr, cols] * cw_ref[j:j + 1, cols]
        zs_ref[:, cols] = (cb_ref[:, cols] * y).astype(BF16)

    @pl.when(t == last)
    def _():
        ps_ref[0] = full_s[POOL_HALO + tr - n_buf:POOL_HALO + tr, :]
        cs_ref[0] = pf_s[CONV_HALO + tr - (taps - 1):CONV_HALO + tr, :]


def _pool_conv_prompt(tail, maps, pscale, conv_w, layer, batch, seq, width, n_buf):
    tr = _tile(seq, 256, POOL_HALO)
    nt = seq // tr
    taps = conv_w.shape[1]
    n_groups, gw = maps.shape[1], maps.shape[2]
    hp = tr // POOL_HALO
    hc = tr // CONV_HALO

    def rows(b, t):
        return b * nt + t

    def main(col):
        return pl.BlockSpec((tr, width), lambda b, t: (rows(b, t), col))

    def halo(col, h, per_tile):
        return pl.BlockSpec((h, width), lambda b, t: (jnp.maximum(rows(b, t) * per_tile - 1, 0), col))

    return pl.pallas_call(
        functools.partial(_pool_conv_prompt_kernel, windows=POOL_WINDOWS),
        grid=(batch, nt),
        in_specs=[main(0), halo(0, POOL_HALO, hp), main(1), main(2), main(3),
                  halo(2, CONV_HALO, hc), halo(3, CONV_HALO, hc),
                  pl.BlockSpec((None, n_groups, gw, gw), lambda b, t: (layer, 0, 0, 0)),
                  pl.BlockSpec((None, 1, width), lambda b, t: (layer, 0, 0)),
                  pl.BlockSpec((None, taps, width), lambda b, t: (layer, 0, 0))],
        out_specs=[pl.BlockSpec((tr, width), lambda b, t: (rows(b, t), 0)),
                   pl.BlockSpec((tr, width), lambda b, t: (rows(b, t), 0)),
                   pl.BlockSpec((1, n_buf, width), lambda b, t: (b, 0, 0)),
                   pl.BlockSpec((1, taps - 1, width), lambda b, t: (b, 0, 0))],
        out_shape=[jax.ShapeDtypeStruct((tail.shape[0], width), BF16),
                   jax.ShapeDtypeStruct((tail.shape[0], width), BF16),
                   jax.ShapeDtypeStruct((batch, n_buf, width), F32),
                   jax.ShapeDtypeStruct((batch, taps - 1, width), F32)],
        scratch_shapes=[pltpu.VMEM((POOL_HALO + tr, width), F32),
                        pltpu.VMEM((CONV_HALO + tr, width), F32)],
        compiler_params=_params(2),
        name="pool_conv_prompt",
    )(tail, tail, tail, tail, tail, tail, tail, maps, pscale, conv_w)


def _pool_conv_sample_kernel(u_ref, cb_ref, cc_ref, cx_ref, ps_ref, cs_ref, maps_ref, scale_ref,
                             cw_ref, zp_ref, zs_ref, pn_ref, cn_ref, *, windows, start_pos):
    g = pl.program_id(0)
    seq_len = u_ref.shape[0]
    n_buf = ps_ref.shape[0]
    taps = cw_ref.shape[0]
    win = jnp.int32(windows[0])
    for i, w in enumerate(windows):
        win = jnp.where(g == i, jnp.int32(w), win)

    hist = [ps_ref[j] for j in range(n_buf)] + [u_ref[t] for t in range(seq_len)]
    for t in range(seq_len):
        cur = n_buf + t
        acc = hist[cur]
        for j in range(1, max(windows)):
            acc = acc + jnp.where(j < win, hist[cur - j], 0.0)
        cnt = jnp.minimum(start_pos + t + 1, win).astype(F32)
        zf = acc / cnt - hist[cur]
        zp = jnp.dot(zf.astype(BF16), maps_ref[0], preferred_element_type=F32) * scale_ref[...]
        zp_ref[t] = zp.astype(BF16)
    for j in range(n_buf):
        pn_ref[j] = hist[seq_len + j]

    prod = [cs_ref[j] for j in range(taps - 1)] + [cc_ref[t] * cx_ref[t] for t in range(seq_len)]
    for t in range(seq_len):
        y = prod[t] * cw_ref[0:1, :]
        for j in range(1, taps):
            y = y + prod[t + j] * cw_ref[j:j + 1, :]
        zs_ref[t] = (cb_ref[t] * y).astype(BF16)
    for j in range(taps - 1):
        cn_ref[j] = prod[seq_len + j]


def _pool_conv_sample(tail_s, pstate_t, cstate_t, maps, pscale, conv_w, layer):
    seq_len, n_seq, _ = tail_s.shape
    n_buf, width = pstate_t.shape[1], pstate_t.shape[3]
    taps = conv_w.shape[1]
    n_groups, gw = maps.shape[1], maps.shape[2]
    assert len(POOL_WINDOWS) == n_groups and n_buf >= max(POOL_WINDOWS) - 1

    def section(k):
        return pl.BlockSpec((seq_len, n_seq, gw), lambda g: (0, 0, k * n_groups + g))

    return pl.pallas_call(
        functools.partial(_pool_conv_sample_kernel, windows=POOL_WINDOWS, start_pos=PAST_LEN),
        grid=(n_groups,),
        in_specs=[section(0), section(1), section(2), section(3),
                  pl.BlockSpec((None, n_buf, n_seq, gw), lambda g: (layer, 0, 0, g)),
                  pl.BlockSpec((None, taps - 1, n_seq, gw), lambda g: (layer, 0, 0, g)),
                  pl.BlockSpec((None, 1, gw, gw), lambda g: (layer, g, 0, 0)),
                  pl.BlockSpec((None, 1, gw), lambda g: (layer, 0, g)),
                  pl.BlockSpec((None, taps, gw), lambda g: (layer, 0, g))],
        out_specs=[pl.BlockSpec((seq_len, n_seq, gw), lambda g: (0, 0, g)),
                   pl.BlockSpec((seq_len, n_seq, gw), lambda g: (0, 0, g)),
                   pl.BlockSpec((n_buf, n_seq, gw), lambda g: (0, 0, g)),
                   pl.BlockSpec((taps - 1, n_seq, gw), lambda g: (0, 0, g))],
        out_shape=[jax.ShapeDtypeStruct((seq_len, n_seq, width), BF16),
                   jax.ShapeDtypeStruct((seq_len, n_seq, width), BF16),
                   jax.ShapeDtypeStruct(pstate_t.shape[1:], F32),
                   jax.ShapeDtypeStruct(cstate_t.shape[1:], F32)],
        compiler_params=_params(1),
        name="pool_conv_sample",
    )(tail_s, tail_s, tail_s, tail_s, pstate_t, cstate_t, maps, pscale, conv_w)


def kernel(x_prompt, x_sample, state_C, state_n, state_m, state_pool, state_conv, w_in, b_in, mh_g, pool_maps, pool_scale, conv_w, w_bm, w_bp, w_bs, w_o, ln1_g, ln1_b, w_gate, w_up, w_down, ln2_g, ln2_b):
    depth = w_in.shape[0]
    batch, seq, d = x_prompt.shape
    n_seq, seq_len, _ = x_sample.shape
    n_heads, dv, dqk = state_C.shape[2], state_C.shape[3], state_C.shape[4]
    n_buf, pw = state_pool.shape[2], state_pool.shape[3]
    cw = conv_w.shape[2]
    dff = w_gate.shape[2]
    assert pw == cw and dv == 2 * dqk and 2 * n_heads <= LANES
    alpha = (2 * depth) ** 0.25
    tp, ts = batch * seq, n_seq * seq_len
    t_all = tp + ts

    off_i = 2 * n_heads * dqk + 2 * n_heads * dv
    off_p = off_i + 2 * n_heads
    gate_col0 = pw + 3 * cw
    k_blocks = 2 if dff % (2 * LANES) == 0 else 1

    w_in_t = jnp.swapaxes(w_in, 1, 2)
    w_if = jnp.pad(w_in[:, :, off_i:off_p], ((0, 0), (0, 0), (0, LANES - (off_p - off_i)))).astype(BF16)
    b3 = b_in[:, None, :]
    b_qkvo, b_tail = b3[:, :, :off_i], b3[:, :, off_p:]
    b_if = jnp.pad(b3[:, :, off_i:off_p], ((0, 0), (0, 0), (0, LANES - (off_p - off_i))))
    wbm, wbp, wbs = (_to_bf16(w) for w in (w_bm, w_bp, w_bs))
    maps = pool_maps.astype(BF16)
    mhg3, pscale3 = mh_g[:, None, :], pool_scale[:, None, :]
    ln1g, ln1b, ln2g, ln2b = (a[:, None, :] for a in (ln1_g, ln1_b, ln2_g, ln2_b))
    state_n5 = state_n[:, :, :, None, :]
    pool_t = state_pool.transpose(0, 2, 1, 3)
    conv_t = state_conv.transpose(0, 2, 1, 3)

    x, xb = _stack_rows(x_prompt.reshape(tp, d), x_sample.transpose(1, 0, 2))

    outs = [[] for _ in range(9)]
    c_s = None
    for l in range(depth):
        qkvo, tail = _in_proj(xb, w_in_t, l, off_i, off_p, w_in.shape[2] - off_p, b_qkvo, b_tail)
        gates = _matmul(xb, w_if, l, bias=b_if, name="in_if")

        hm, c_p, n_p, m_p = _mlstm_prompt(qkvo, gates, mhg3, l, batch, seq, t_all, n_heads, dqk, dv)
        m_rep = jnp.broadcast_to(state_m[l][None], (seq_len, n_seq, n_heads))
        hm_s, c_s, n_s, m_s = _mlstm_sample(
            qkvo[tp:].reshape(seq_len, n_seq, -1), gates[tp:].reshape(seq_len, n_seq, -1), mhg3,
            state_C, state_n5, m_rep, c_s, l, n_heads, dqk, dv)
        hm = lax.dynamic_update_slice(hm, hm_s.reshape(ts, -1), (tp, 0))

        zp, zs, ps_p, cs_p = _pool_conv_prompt(tail, maps, pscale3, conv_w, l, batch, seq, pw, n_buf)
        tail_s = tail[tp:, :pw + 3 * cw].reshape(seq_len, n_seq, -1)
        zp_s, zs_s, ps_s, cs_s = _pool_conv_sample(tail_s, pool_t, conv_t, maps, pscale3, conv_w, l)
        zp = lax.dynamic_update_slice(zp, zp_s.reshape(ts, pw), (tp, 0))
        zs = lax.dynamic_update_slice(zs, zs_s.reshape(ts, cw), (tp, 0))

        mixed = _merge(hm, zp, zs, wbm, wbp, wbs, l, tail, gate_col0)
        r1 = _matmul(mixed, w_o, l, res=x, res_scale=alpha, name="out_proj")
        x1, x1b = _layernorm(r1, ln1g, ln1b, l)

        ff = _ffn_up(x1b, w_gate, w_up, l, tn=FF_TILE)
        r2 = _matmul(ff, w_down, l, res=x1, res_scale=alpha, tn=FF_TILE, k_blocks=k_blocks,
                     name="ffn_down")
        for kb in range(1, k_blocks):
            r2 = _matmul(ff, w_down, l, res=r2, tn=FF_TILE, k_blocks=k_blocks, k_index=kb,
                         name="ffn_down")
        if l + 1 < depth:
            x, xb = _layernorm(r2, ln2g, ln2b, l)
        else:
            y_p, = _layernorm(r2, ln2g, ln2b, l, 0, tp, with_bf16=False)
            y_s, = _layernorm(r2, ln2g, ln2b, l, tp, ts, with_bf16=False, time_major_seqs=n_seq)

        new = (c_p, n_p, m_p[:, :, 0], ps_p, cs_p,
               n_s[:, :, 0, :], m_s[:, seq_len - 1, :, 0].T,
               ps_s.transpose(1, 0, 2), cs_s.transpose(1, 0, 2))
        for acc, val in zip(outs, new):
            acc.append(val)

    st = [jnp.stack(o) for o in outs]
    return (y_p.reshape(batch, seq, d), y_s.transpose(1, 0, 2),
            st[0], st[1], st[2], st[3], st[4], c_s, st[5], st[6], st[7], st[8])
```
